```python
import math
import jax, jax.numpy as jnp
from jax import lax
import numpy as np

D_MODEL = 4096
BATCH = 1
SEQ = 8192
DEPTH = 4

N_A_LAYERS = DEPTH // 2
N_B_LAYERS = DEPTH - N_A_LAYERS
POOL_WIDTH = D_MODEL
POOL_WINDOWS = (2, 4, 8, 16)
N_POOL_GROUPS = len(POOL_WINDOWS)
POOL_GROUP_DIM = POOL_WIDTH // N_POOL_GROUPS
SB_HEAD_DIM = 128
SB_N_HEADS = D_MODEL // SB_HEAD_DIM
SB_WIDTH = SB_N_HEADS * SB_HEAD_DIM
Q_BLOCK = 128
RMS_EPS = 1e-6

kernel_name = "yoco_pool_stickbreaking_hybrid"


def rms_norm(x, g):
    xf = x.astype(jnp.float32)
    y = xf * lax.rsqrt(jnp.mean(xf * xf, axis=-1, keepdims=True) + RMS_EPS)
    return (y * g.astype(jnp.float32)).astype(x.dtype)


def causal_multiscale_pool(u):
    b, s, _, c = u.shape
    uf = u.astype(jnp.float32)
    csum = jnp.cumsum(uf, axis=1)
    pos = jnp.arange(s)
    outs = []
    for g, w in enumerate(POOL_WINDOWS):
        cg = csum[:, :, g]
        shifted = jnp.pad(cg, ((0, 0), (w, 0), (0, 0)))[:, :s]
        count = jnp.minimum(pos + 1, w).astype(jnp.float32)[None, :, None]
        outs.append((cg - shifted) / count - uf[:, :, g])
    return jnp.stack(outs, axis=2).astype(u.dtype)


def pool_layer(x, pre_g, w_in, w_group, scale, w_out, post_g):
    b, s, _ = x.shape
    h = rms_norm(x, pre_g)
    ug = h @ w_in
    u, gate = jnp.split(ug, 2, axis=-1)
    u = u.reshape(b, s, N_POOL_GROUPS, POOL_GROUP_DIM)
    pooled = causal_multiscale_pool(u)
    mixed = jnp.einsum('bsgc,gcd->bsgd', pooled, w_group).reshape(b, s, POOL_WIDTH)
    y = (mixed * scale * jax.nn.silu(gate)) @ w_out
    return x + rms_norm(y, post_g)


def stick_breaking_attention(q, k, v):
    b, h, s, d = q.shape
    nb = s // Q_BLOCK
    q_blocks = q.reshape(b, h, nb, Q_BLOCK, d).transpose(2, 0, 1, 3, 4)
    kf = k.astype(jnp.float32)
    vf = v.astype(jnp.float32)
    key_pos = jnp.arange(s)
    inv_sqrt_d = 1.0 / math.sqrt(d)

    def one_block(args):
        qb, i = args
        q_pos = i * Q_BLOCK + jnp.arange(Q_BLOCK)
        z = jnp.einsum('bhqd,bhkd->bhqk', qb.astype(jnp.float32), kf) * inv_sqrt_d
        mask = key_pos[None, :] < q_pos[:, None]
        log_1mb = jnp.where(mask, jax.nn.log_sigmoid(-z), 0.0)
        after = lax.cumsum(log_1mb, axis=3, reverse=True) - log_1mb
        a = jnp.where(mask, jnp.exp(jax.nn.log_sigmoid(z) + after), 0.0)
        return jnp.einsum('bhqk,bhkd->bhqd', a, vf)

    out = lax.map(one_block, (q_blocks, jnp.arange(nb)))
    return out.transpose(1, 2, 0, 3, 4).reshape(b, h, s, d).astype(q.dtype)


def sb_layer(x, k, v, pre_g, w_in, w_out, post_g):
    b, s, _ = x.shape
    h = rms_norm(x, pre_g)
    qg = h @ w_in
    q, gate = jnp.split(qg, 2, axis=-1)
    q = q.reshape(b, s, SB_N_HEADS, SB_HEAD_DIM).transpose(0, 2, 1, 3)
    o = stick_breaking_attention(q, k, v)
    o = o.transpose(0, 2, 1, 3).reshape(b, s, SB_WIDTH)
    y = (o * jax.nn.silu(gate)) @ w_out
    return x + rms_norm(y, post_g)


def setup_inputs(seed: int = 0) -> dict:
    key = jax.random.key(seed)
    ks = jax.random.split(key, 16)
    f32 = jnp.float32

    def dense(k, shape, fan_in):
        return jax.random.normal(k, shape, f32) * (fan_in ** -0.5)

    def gain(k, shape):
        return 1.0 + 0.05 * jax.random.normal(k, shape, f32)

    return {
        "x": jax.random.normal(ks[0], (BATCH, SEQ, D_MODEL), f32),
        "a_pre_norm": gain(ks[1], (N_A_LAYERS, D_MODEL)),
        "a_w_in": dense(ks[2], (N_A_LAYERS, D_MODEL, 2 * POOL_WIDTH), D_MODEL),
        "a_w_group": dense(ks[3], (N_A_LAYERS, N_POOL_GROUPS, POOL_GROUP_DIM, POOL_GROUP_DIM), POOL_GROUP_DIM),
        "a_scale": gain(ks[4], (N_A_LAYERS, POOL_WIDTH)),
        "a_w_out": dense(ks[5], (N_A_LAYERS, POOL_WIDTH, D_MODEL), POOL_WIDTH),
        "a_post_norm": gain(ks[6], (N_A_LAYERS, D_MODEL)),
        "kv_norm": gain(ks[7], (D_MODEL,)),
        "w_kv": dense(ks[8], (D_MODEL, 2 * SB_WIDTH), D_MODEL),
        "b_pre_norm": gain(ks[9], (N_B_LAYERS, D_MODEL)),
        "b_w_in": dense(ks[10], (N_B_LAYERS, D_MODEL, 2 * SB_WIDTH), D_MODEL),
        "b_w_out": dense(ks[11], (N_B_LAYERS, SB_WIDTH, D_MODEL), SB_WIDTH),
        "b_post_norm": gain(ks[12], (N_B_LAYERS, D_MODEL)),
    }


def reference(x, a_pre_norm, a_w_in, a_w_group, a_scale, a_w_out, a_post_norm,
              kv_norm, w_kv, b_pre_norm, b_w_in, b_w_out, b_post_norm):
    b, s, _ = x.shape
    k = None
    v = None
    for layer in range(DEPTH):
        if layer < N_A_LAYERS:
            x = pool_layer(x, a_pre_norm[layer], a_w_in[layer], a_w_group[layer],
                           a_scale[layer], a_w_out[layer], a_post_norm[layer])
            if layer == N_A_LAYERS - 1:
                kv = rms_norm(x, kv_norm) @ w_kv
                k, v = jnp.split(kv, 2, axis=-1)
                k = k.reshape(b, s, SB_N_HEADS, SB_HEAD_DIM).transpose(0, 2, 1, 3)
                v = v.reshape(b, s, SB_N_HEADS, SB_HEAD_DIM).transpose(0, 2, 1, 3)
        else:
            j = layer - N_A_LAYERS
            x = sb_layer(x, k, v, b_pre_norm[j], b_w_in[j], b_w_out[j], b_post_norm[j])
    return x
```

```python
import functools
import math

import jax
import jax.numpy as jnp
from jax import lax
from jax.experimental import pallas as pl
from jax.experimental.pallas import tpu as pltpu

RMS_EPS = 1e-6
POOL_WINDOWS = (2, 4, 8, 16)
HEAD_DIM = 128

V7X_VMEM_BYTES = 64 * 1024 * 1024
V7X_LANES = 128

BF16 = jnp.bfloat16
F32 = jnp.float32


def _vmem_limit(nbytes):
    return int(min(V7X_VMEM_BYTES - 4 * 1024 * 1024, nbytes + 12 * 1024 * 1024))


def _norm_matmul_kernel(x_ref, g_ref, w_ref, o_ref, h_ref, *, q_tiles, q_scale, row_chunk):
    j = pl.program_id(1)
    tm, d = x_ref.shape

    @pl.when(j == 0)
    def _():
        g = g_ref[...]

        def body(c, _):
            r = pl.ds(pl.multiple_of(c * row_chunk, row_chunk), row_chunk)
            xs = x_ref[r, :]
            ms = jnp.mean(xs * xs, axis=-1, keepdims=True)
            h_ref[r, :] = (xs * lax.rsqrt(ms + RMS_EPS) * g).astype(h_ref.dtype)
            return 0

        lax.fori_loop(0, tm // row_chunk, body, 0)

    acc = jnp.dot(h_ref[...], w_ref[...], preferred_element_type=F32)
    if q_tiles:
        acc = acc * jnp.where(j < q_tiles, q_scale, 1.0).astype(F32)
    o_ref[...] = acc.astype(o_ref.dtype)


def _norm_matmul(x, g, w, *, q_scale=None, tm=512, tn=1024):
    s, d = x.shape
    n = w.shape[1]
    q_tiles = (n // 2) // tn if q_scale is not None else 0
    kern = functools.partial(_norm_matmul_kernel, q_tiles=q_tiles,
                             q_scale=q_scale if q_scale is not None else 1.0, row_chunk=32)
    vmem = 2 * tm * d * 4 + tm * d * 2 + 2 * d * tn * 2 + 2 * tm * tn * 2 + tm * tn * 4
    return pl.pallas_call(
        kern,
        grid=(s // tm, n // tn),
        in_specs=[
            pl.BlockSpec((tm, d), lambda i, j: (i, 0)),
            pl.BlockSpec((1, d), lambda i, j: (0, 0)),
            pl.BlockSpec((d, tn), lambda i, j: (0, j)),
        ],
        out_specs=pl.BlockSpec((tm, tn), lambda i, j: (i, j)),
        out_shape=jax.ShapeDtypeStruct((s, n), BF16),
        scratch_shapes=[pltpu.VMEM((tm, d), BF16)],
        compiler_params=pltpu.CompilerParams(
            dimension_semantics=("arbitrary", "arbitrary"),
            vmem_limit_bytes=_vmem_limit(vmem)),
        name="norm_matmul",
    )(x, g.reshape(1, d), w)


POOL_SUB = 128


def _pool_kernel(u_ref, halo_ref, gate_ref, wg_ref, scale_ref, o_ref, pooled_ref):
    i = pl.program_id(0)
    g = pl.program_id(1)
    tm = u_ref.shape[0]
    w = lax.shift_left(jnp.int32(2), g)

    r = lax.broadcasted_iota(jnp.int32, (POOL_SUB, POOL_SUB), 0)
    c = lax.broadcasted_iota(jnp.int32, (POOL_SUB, POOL_SUB), 1)
    d_cur = r - c
    d_prev = d_cur + POOL_SUB
    band_cur = ((d_cur >= 0) & (d_cur < w)).astype(BF16)
    band_prev = (d_prev < w).astype(BF16)

    for rb in range(tm // POOL_SUB):
        rows = slice(rb * POOL_SUB, (rb + 1) * POOL_SUB)
        cur = u_ref[rows, :]
        if rb == 0:
            halo = halo_ref[...]
            prev = jnp.where(i > 0, halo, jnp.zeros_like(halo))
        else:
            prev = u_ref[(rb - 1) * POOL_SUB:rb * POOL_SUB, :]
        wsum = (jnp.dot(band_cur, cur, preferred_element_type=F32)
                + jnp.dot(band_prev, prev, preferred_element_type=F32))
        t = i * tm + rb * POOL_SUB + lax.broadcasted_iota(jnp.int32, (POOL_SUB, 1), 0)
        inv_count = 1.0 / jnp.minimum(t + 1, w).astype(F32)
        pooled_ref[rows, :] = (wsum * inv_count - cur.astype(F32)).astype(pooled_ref.dtype)

    mixed = jnp.dot(pooled_ref[...], wg_ref[0], preferred_element_type=F32)
    gate = gate_ref[...].astype(F32)
    o_ref[...] = (mixed * scale_ref[...] * (gate * jax.nn.sigmoid(gate))).astype(o_ref.dtype)


def _pool_mix(ug, w_group, scale, *, tm=512):
    s = ug.shape[0]
    n_groups, gd, _ = w_group.shape
    width = n_groups * gd
    sub_per_tile = tm // POOL_SUB
    vmem = 2 * (2 * tm * gd * 2 + POOL_SUB * gd * 2 + gd * gd * 2 + tm * gd * 2) + tm * gd * 2 + 2 * tm * gd * 4
    return pl.pallas_call(
        _pool_kernel,
        grid=(s // tm, n_groups),
        in_specs=[
            pl.BlockSpec((tm, gd), lambda i, g: (i, g)),
            pl.BlockSpec((POOL_SUB, gd), lambda i, g: (jnp.maximum(i * sub_per_tile - 1, 0), g)),
            pl.BlockSpec((tm, gd), lambda i, g: (i, n_groups + g)),
            pl.BlockSpec((1, gd, gd), lambda i, g: (g, 0, 0)),
            pl.BlockSpec((1, gd), lambda i, g: (0, g)),
        ],
        out_specs=pl.BlockSpec((tm, gd), lambda i, g: (i, g)),
        out_shape=jax.ShapeDtypeStruct((s, width), BF16),
        scratch_shapes=[pltpu.VMEM((tm, gd), BF16)],
        compiler_params=pltpu.CompilerParams(
            dimension_semantics=("arbitrary", "arbitrary"),
            vmem_limit_bytes=_vmem_limit(vmem)),
        name="pool_mix",
    )(ug, ug, ug, w_group, scale.reshape(1, width))


def _out_proj_kernel(a_ref, w_ref, x_ref, g_ref, o_ref, *, col_chunk, row_chunk):
    k = pl.program_id(1)
    nk = pl.num_programs(1)
    tm, d = o_ref.shape
    a = a_ref[...]

    for cc in range(d // col_chunk):
        cols = slice(cc * col_chunk, (cc + 1) * col_chunk)
        p = jnp.dot(a, w_ref[:, cols], preferred_element_type=F32)

        @pl.when(k == 0)
        def _():
            o_ref[:, cols] = p

        @pl.when(k > 0)
        def _():
            o_ref[:, cols] += p

    @pl.when(k == nk - 1)
    def _():
        g = g_ref[...]

        def body(c, _):
            r = pl.ds(pl.multiple_of(c * row_chunk, row_chunk), row_chunk)
            y = o_ref[r, :]
            ms = jnp.mean(y * y, axis=-1, keepdims=True)
            o_ref[r, :] = x_ref[r, :] + y * lax.rsqrt(ms + RMS_EPS) * g
            return 0

        lax.fori_loop(0, tm // row_chunk, body, 0)


def _out_proj(a, w, x, g, *, tm=512, tk=512):
    s, kdim = a.shape
    d = w.shape[1]
    kern = functools.partial(_out_proj_kernel, col_chunk=512, row_chunk=32)
    vmem = 2 * tm * tk * 2 + 2 * tk * d * 2 + 4 * tm * d * 4 + tm * 512 * 4
    return pl.pallas_call(
        kern,
        grid=(s // tm, kdim // tk),
        in_specs=[
            pl.BlockSpec((tm, tk), lambda i, k: (i, k)),
            pl.BlockSpec((tk, d), lambda i, k: (k, 0)),
            pl.BlockSpec((tm, d), lambda i, k: (i, 0)),
            pl.BlockSpec((1, d), lambda i, k: (0, 0)),
        ],
        out_specs=pl.BlockSpec((tm, d), lambda i, k: (i, 0)),
        out_shape=jax.ShapeDtypeStruct((s, d), F32),
        compiler_params=pltpu.CompilerParams(
            dimension_semantics=("arbitrary", "arbitrary"),
            vmem_limit_bytes=_vmem_limit(vmem)),
        name="out_proj",
    )(a, w, x, g.reshape(1, d))


def _softplus(z):
    return jnp.maximum(z, 0.0) + jnp.log(1.0 + jnp.exp(-jnp.abs(z)))


def _sb_attn_kernel(q_ref, k_ref, v_ref, gate_ref, o_ref, acc_ref, carry_ref, *, blk):
    i = pl.program_id(1)
    q = q_ref[...]
    r = lax.broadcasted_iota(jnp.int32, (blk, blk), 0)
    c = lax.broadcasted_iota(jnp.int32, (blk, blk), 1)
    suffix = (r >= c).astype(BF16)
    causal = c < r

    def step(kb, masked):
        rows = pl.ds(pl.multiple_of(kb * blk, blk), blk)
        ks = k_ref[rows, :]
        vs = v_ref[rows, :]
        z = lax.dot_general(q, ks, (((1,), (1,)), ((), ())), preferred_element_type=F32)
        sp = _softplus(z)
        if masked:
            sp = jnp.where(causal, sp, 0.0)
        total = jnp.dot(sp.astype(BF16), suffix, preferred_element_type=F32) + carry_ref[...]
        a = jnp.exp(z - total)
        if masked:
            a = jnp.where(causal, a, 0.0)
        acc_ref[...] += jnp.dot(a.astype(BF16), vs, preferred_element_type=F32)
        carry_ref[...] = total[:, 0:1]

    acc_ref[...] = jnp.zeros_like(acc_ref)
    carry_ref[...] = jnp.zeros_like(carry_ref)
    step(i, True)

    def body(n, _):
        step(i - 1 - n, False)
        return 0

    lax.fori_loop(0, i, body, 0)

    gate = gate_ref[...].astype(F32)
    o_ref[...] = (acc_ref[...] * (gate * jax.nn.sigmoid(gate))).astype(o_ref.dtype)


def _sb_attention(qg, kv, *, blk=256):
    s, two_w = qg.shape
    n_heads = (two_w // 2) // HEAD_DIM
    kern = functools.partial(_sb_attn_kernel, blk=blk)
    vmem = 2 * (2 * s * HEAD_DIM * 2 + 3 * blk * HEAD_DIM * 2) + 8 * blk * blk * 4
    return pl.pallas_call(
        kern,
        grid=(n_heads, s // blk),
        in_specs=[
            pl.BlockSpec((blk, HEAD_DIM), lambda h, i: (i, h)),
            pl.BlockSpec((s, HEAD_DIM), lambda h, i: (0, h)),
            pl.BlockSpec((s, HEAD_DIM), lambda h, i: (0, n_heads + h)),
            pl.BlockSpec((blk, HEAD_DIM), lambda h, i: (i, n_heads + h)),
        ],
        out_specs=pl.BlockSpec((blk, HEAD_DIM), lambda h, i: (i, h)),
        out_shape=jax.ShapeDtypeStruct((s, n_heads * HEAD_DIM), BF16),
        scratch_shapes=[pltpu.VMEM((blk, HEAD_DIM), F32), pltpu.VMEM((blk, 1), F32)],
        compiler_params=pltpu.CompilerParams(
            dimension_semantics=("arbitrary", "arbitrary"),
            vmem_limit_bytes=_vmem_limit(vmem)),
        name="sb_attention",
    )(qg, kv, kv, qg)


def kernel(x, a_pre_norm, a_w_in, a_w_group, a_scale, a_w_out, a_post_norm, kv_norm, w_kv,
           b_pre_norm, b_w_in, b_w_out, b_post_norm):
    b, s, d = x.shape
    xs = x.reshape(b * s, d)
    for layer in range(a_w_in.shape[0]):
        ug = _norm_matmul(xs, a_pre_norm[layer], a_w_in[layer].astype(BF16))
        mixed = _pool_mix(ug, a_w_group[layer].astype(BF16), a_scale[layer])
        xs = _out_proj(mixed, a_w_out[layer].astype(BF16), xs, a_post_norm[layer])
    kv = _norm_matmul(xs, kv_norm, w_kv.astype(BF16))
    for layer in range(b_w_in.shape[0]):
        qg = _norm_matmul(xs, b_pre_norm[layer], b_w_in[layer].astype(BF16),
                          q_scale=1.0 / math.sqrt(HEAD_DIM))
        og = _sb_attention(qg, kv)
        xs = _out_proj(og, b_w_out[layer].astype(BF16), xs, b_post_norm[layer])
    return xs.reshape(b, s, d)
```

```python
import functools
import math

import jax
import jax.numpy as jnp
from jax import lax
from jax.experimental import pallas as pl
from jax.experimental.pallas import tpu as pltpu

RMS_EPS = 1e-6
POOL_WINDOWS = (2, 4, 8, 16)
HEAD_DIM = 128

V7X_VMEM_BYTES = 64 * 1024 * 1024
V7X_LANES = 128

BF16 = jnp.bfloat16
F32 = jnp.float32


def _vmem_limit(nbytes):
    return int(min(V7X_VMEM_BYTES - 4 * 1024 * 1024, nbytes + 12 * 1024 * 1024))


def _norm_matmul_kernel(x_ref, g_ref, w_ref, o_ref, h_ref, *, q_tiles, q_scale, row_chunk):
    j = pl.program_id(1)
    tm, d = x_ref.shape

    @pl.when(j == 0)
    def _():
        g = g_ref[...]

        def body(c, _):
            r = pl.ds(pl.multiple_of(c * row_chunk, row_chunk), row_chunk)
            xs = x_ref[r, :]
            ms = jnp.mean(xs * xs, axis=-1, keepdims=True)
            h_ref[r, :] = (xs * lax.rsqrt(ms + RMS_EPS) * g).astype(h_ref.dtype)
            return 0

        lax.fori_loop(0, tm // row_chunk, body, 0)

    acc = jnp.dot(h_ref[...], w_ref[...], preferred_element_type=F32)
    if q_tiles:
        acc = acc * jnp.where(j < q_tiles, q_scale, 1.0).astype(F32)
    o_ref[...] = acc.astype(o_ref.dtype)


def _norm_matmul(x, g, w, *, q_scale=None, tm=512, tn=1024):
    s, d = x.shape
    n = w.shape[1]
    q_tiles = (n // 2) // tn if q_scale is not None else 0
    kern = functools.partial(_norm_matmul_kernel, q_tiles=q_tiles,
                             q_scale=q_scale if q_scale is not None else 1.0, row_chunk=32)
    vmem = 2 * tm * d * 4 + tm * d * 2 + 2 * d * tn * 2 + 2 * tm * tn * 2 + tm * tn * 4
    return pl.pallas_call(
        kern,
        grid=(s // tm, n // tn),
        in_specs=[
            pl.BlockSpec((tm, d), lambda i, j: (i, 0)),
            pl.BlockSpec((1, d), lambda i, j: (0, 0)),
            pl.BlockSpec((d, tn), lambda i, j: (0, j)),
        ],
        out_specs=pl.BlockSpec((tm, tn), lambda i, j: (i, j)),
        out_shape=jax.ShapeDtypeStruct((s, n), BF16),
        scratch_shapes=[pltpu.VMEM((tm, d), BF16)],
        compiler_params=pltpu.CompilerParams(
            dimension_semantics=("arbitrary", "arbitrary"),
            vmem_limit_bytes=_vmem_limit(vmem)),
        name="norm_matmul",
    )(x, g.reshape(1, d), w)


POOL_SUB = 128


def _pool_kernel(u_ref, halo_ref, gate_ref, wg_ref, scale_ref, o_ref, pooled_ref):
    i = pl.program_id(0)
    g = pl.program_id(1)
    tm = u_ref.shape[0]
    w = lax.shift_left(jnp.int32(2), g)

    r = lax.broadcasted_iota(jnp.int32, (POOL_SUB, POOL_SUB), 0)
    c = lax.broadcasted_iota(jnp.int32, (POOL_SUB, POOL_SUB), 1)
    d_cur = r - c
    d_prev = d_cur + POOL_SUB
    band_cur = ((d_cur >= 0) & (d_cur < w)).astype(BF16)
    band_prev = (d_prev < w).astype(BF16)

    for rb in range(tm // POOL_SUB):
        rows = slice(rb * POOL_SUB, (rb + 1) * POOL_SUB)
        cur = u_ref[rows, :]
        if rb == 0:
            halo = halo_ref[...]
            prev = jnp.where(i > 0, halo, jnp.zeros_like(halo))
        else:
            prev = u_ref[(rb - 1) * POOL_SUB:rb * POOL_SUB, :]
        wsum = (jnp.dot(band_cur, cur, preferred_element_type=F32)
                + jnp.dot(band_prev, prev, preferred_element_type=F32))
        t = i * tm + rb * POOL_SUB + lax.broadcasted_iota(jnp.int32, (POOL_SUB, 1), 0)
        inv_count = 1.0 / jnp.minimum(t + 1, w).astype(F32)
        pooled_ref[rows, :] = (wsum * inv_count - cur.astype(F32)).astype(pooled_ref.dtype)

    mixed = jnp.dot(pooled_ref[...], wg_ref[0], preferred_element_type=F32)
    gate = gate_ref[...].astype(F32)
    o_ref[...] = (mixed * scale_ref[...] * (gate * jax.nn.sigmoid(gate))).astype(o_ref.dtype)


def _pool_mix(ug, w_group, scale, *, tm=512):
    s = ug.shape[0]
    n_groups, gd, _ = w_group.shape
    width = n_groups * gd
    sub_per_tile = tm // POOL_SUB
    vmem = 2 * (2 * tm * gd * 2 + POOL_SUB * gd * 2 + gd * gd * 2 + tm * gd * 2) + tm * gd * 2 + 2 * tm * gd * 4
    return pl.pallas_call(
        _pool_kernel,
        grid=(s // tm, n_groups),
        in_specs=[
            pl.BlockSpec((tm, gd), lambda i, g: (i, g)),
            pl.BlockSpec((POOL_SUB, gd), lambda i, g: (jnp.maximum(i * sub_per_tile - 1, 0), g)),
            pl.BlockSpec((tm, gd), lambda i, g: (i, n_groups + g)),
            pl.BlockSpec((1, gd, gd), lambda i, g: (g, 0, 0)),
            pl.BlockSpec((1, gd), lambda i, g: (0, g)),
        ],
        out_specs=pl.BlockSpec((tm, gd), lambda i, g: (i, g)),
        out_shape=jax.ShapeDtypeStruct((s, width), BF16),
        scratch_shapes=[pltpu.VMEM((tm, gd), BF16)],
        compiler_params=pltpu.CompilerParams(
            dimension_semantics=("arbitrary", "arbitrary"),
            vmem_limit_bytes=_vmem_limit(vmem)),
        name="pool_mix",
    )(ug, ug, ug, w_group, scale.reshape(1, width))


def _out_proj_kernel(a_ref, w_ref, x_ref, g_ref, o_ref, *, col_chunk, row_chunk):
    k = pl.program_id(1)
    nk = pl.num_programs(1)
    tm, d = o_ref.shape
    a = a_ref[...]

    for cc in range(d // col_chunk):
        cols = slice(cc * col_chunk, (cc + 1) * col_chunk)
        p = jnp.dot(a, w_ref[:, cols], preferred_element_type=F32)

        @pl.when(k == 0)
        def _():
            o_ref[:, cols] = p

        @pl.when(k > 0)
        def _():
            o_ref[:, cols] += p

    @pl.when(k == nk - 1)
    def _():
        g = g_ref[...]

        def body(c, _):
            r = pl.ds(pl.multiple_of(c * row_chunk, row_chunk), row_chunk)
            y = o_ref[r, :]
            ms = jnp.mean(y * y, axis=-1, keepdims=True)
            o_ref[r, :] = x_ref[r, :] + y * lax.rsqrt(ms + RMS_EPS) * g
            return 0

        lax.fori_loop(0, tm // row_chunk, body, 0)


def _out_proj(a, w, x, g, *, tm=512, tk=512):
    s, kdim = a.shape
    d = w.shape[1]
    kern = functools.partial(_out_proj_kernel, col_chunk=512, row_chunk=32)
    vmem = 2 * tm * tk * 2 + 2 * tk * d * 2 + 4 * tm * d * 4 + tm * 512 * 4
    return pl.pallas_call(
        kern,
        grid=(s // tm, kdim // tk),
        in_specs=[
            pl.BlockSpec((tm, tk), lambda i, k: (i, k)),
            pl.BlockSpec((tk, d), lambda i, k: (k, 0)),
            pl.BlockSpec((tm, d), lambda i, k: (i, 0)),
            pl.BlockSpec((1, d), lambda i, k: (0, 0)),
        ],
        out_specs=pl.BlockSpec((tm, d), lambda i, k: (i, 0)),
        out_shape=jax.ShapeDtypeStruct((s, d), F32),
        compiler_params=pltpu.CompilerParams(
            dimension_semantics=("arbitrary", "arbitrary"),
            vmem_limit_bytes=_vmem_limit(vmem)),
        name="out_proj",
    )(a, w, x, g.reshape(1, d))


LOG2_E = math.log2(math.e)
SOFTPLUS2_LINEAR_ABOVE = 64.0
MASKED_EXPONENT = -1e30
SKIP_CARRY = 160.0


def _softplus2(z):
    return jnp.where(z > SOFTPLUS2_LINEAR_ABOVE, z, jnp.log(1.0 + jnp.exp2(z)) * LOG2_E)


def _sb_attn_kernel(q_ref, k_ref, v_ref, gate_ref, o_ref, acc_ref, carry_ref, *, tq, bk):
    i = pl.program_id(1)
    n_blk = tq // bk
    q = q_ref[...]
    r = lax.broadcasted_iota(jnp.int32, (bk, bk), 0)
    c = lax.broadcasted_iota(jnp.int32, (bk, bk), 1)
    suffix = (r >= c).astype(BF16)

    def key_rows(step, b):
        kb = (i - step) * n_blk + (n_blk - 1 - b)
        return pl.ds(pl.multiple_of(kb * bk, bk), bk)

    def scores(rows, masked):
        z = lax.dot_general(q, k_ref[rows, :], (((1,), (1,)), ((), ())), preferred_element_type=F32)
        sp = _softplus2(z)
        if masked:
            t_pos = i * tq + lax.broadcasted_iota(jnp.int32, (tq, bk), 0)
            s_pos = rows.start + lax.broadcasted_iota(jnp.int32, (tq, bk), 1)
            causal = s_pos < t_pos
            sp = jnp.where(causal, sp, 0.0)
        s_in = jnp.dot(sp.astype(BF16), suffix, preferred_element_type=F32)
        expo = jnp.minimum(z - s_in, 0.0)
        if masked:
            expo = jnp.where(causal, expo, MASKED_EXPONENT)
        return expo, s_in[:, 0:1]

    def accumulate(rows, expo, tot):
        carry = carry_ref[...]
        a = jnp.exp2(expo - carry)
        acc_ref[...] += jnp.dot(a.astype(BF16), v_ref[rows, :], preferred_element_type=F32)
        carry_ref[...] = carry + tot

    def run_step(step, masked):
        rows = [key_rows(step, b) for b in range(n_blk)]
        parts = [scores(rw, masked) for rw in rows]
        for rw, (expo, tot) in zip(rows, parts):
            accumulate(rw, expo, tot)

    acc_ref[...] = jnp.zeros_like(acc_ref)
    carry_ref[...] = jnp.zeros_like(carry_ref)
    run_step(0, True)

    def keep_going(state):
        step, min_carry = state
        return (step <= i) & (min_carry < SKIP_CARRY)

    def body(state):
        step, _ = state
        run_step(step, False)
        return step + 1, jnp.min(carry_ref[...])

    lax.while_loop(keep_going, body, (jnp.int32(1), jnp.float32(0.0)))

    gate = gate_ref[...].astype(F32)
    o_ref[...] = (acc_ref[...] * (gate * jax.nn.sigmoid(gate))).astype(o_ref.dtype)


def _sb_attention(qg, kv, *, tq=512, bk=256):
    s, two_w = qg.shape
    n_heads = (two_w // 2) // HEAD_DIM
    assert tq % bk == 0
    kern = functools.partial(_sb_attn_kernel, tq=tq, bk=bk)
    vmem = 2 * (2 * s * HEAD_DIM * 2 + 3 * tq * HEAD_DIM * 2) + 16 * tq * bk * 4
    return pl.pallas_call(
        kern,
        grid=(n_heads, s // tq),
        in_specs=[
            pl.BlockSpec((tq, HEAD_DIM), lambda h, i: (i, h)),
            pl.BlockSpec((s, HEAD_DIM), lambda h, i: (0, h)),
            pl.BlockSpec((s, HEAD_DIM), lambda h, i: (0, n_heads + h)),
            pl.BlockSpec((tq, HEAD_DIM), lambda h, i: (i, n_heads + h)),
        ],
        out_specs=pl.BlockSpec((tq, HEAD_DIM), lambda h, i: (i, h)),
        out_shape=jax.ShapeDtypeStruct((s, n_heads * HEAD_DIM), BF16),
        scratch_shapes=[pltpu.VMEM((tq, HEAD_DIM), F32), pltpu.VMEM((tq, 1), F32)],
        compiler_params=pltpu.CompilerParams(
            dimension_semantics=("arbitrary", "arbitrary"),
            vmem_limit_bytes=_vmem_limit(vmem)),
        name="sb_attention",
    )(qg, kv, kv, qg)


def kernel(x, a_pre_norm, a_w_in, a_w_group, a_scale, a_w_out, a_post_norm, kv_norm, w_kv,
           b_pre_norm, b_w_in, b_w_out, b_post_norm):
    b, s, d = x.shape
    xs = x.reshape(b * s, d)
    for layer in range(a_w_in.shape[0]):
        ug = _norm_matmul(xs, a_pre_norm[layer], a_w_in[layer].astype(BF16))
        mixed = _pool_mix(ug, a_w_group[layer].astype(BF16), a_scale[layer])
        xs = _out_proj(mixed, a_w_out[layer].astype(BF16), xs, a_post_norm[layer])
    kv = _norm_matmul(xs, kv_norm, w_kv.astype(BF16))
    for layer in range(b_w_in.shape[0]):
        qg = _norm_matmul(xs, b_pre_norm[layer], b_w_in[layer].astype(BF16),
                          q_scale=LOG2_E / math.sqrt(HEAD_DIM))
        og = _sb_attention(qg, kv)
        xs = _out_proj(og, b_w_out[layer].astype(BF16), xs, b_post_norm[layer])
    return xs.reshape(b, s, d)
```

```python
import functools
import math

import jax
import jax.numpy as jnp
from jax import lax
from jax.experimental import pallas as pl
from jax.experimental.pallas import tpu as pltpu

RMS_EPS = 1e-6
HEAD_DIM = 128

V7X_VMEM_BYTES = 64 * 1024 * 1024

BF16 = jnp.bfloat16
F32 = jnp.float32
NORM_ROW_CHUNK = 32


def _vmem_limit(nbytes):
    return int(min(V7X_VMEM_BYTES - 6 * 1024 * 1024, nbytes + 8 * 1024 * 1024))


def _rms_scale(x):
    return lax.rsqrt(jnp.mean(x * x, axis=-1, keepdims=True) + RMS_EPS)


def _norm_kernel(x_ref, g_ref, o_ref):
    x = x_ref[...]
    o_ref[...] = (x * _rms_scale(x) * g_ref[...]).astype(o_ref.dtype)


def _norm(x, g, *, tm=128):
    s, d = x.shape
    return pl.pallas_call(
        _norm_kernel,
        grid=(s // tm,),
        in_specs=[pl.BlockSpec((tm, d), lambda i: (i, 0)), pl.BlockSpec((1, d), lambda i: (0, 0))],
        out_specs=pl.BlockSpec((tm, d), lambda i: (i, 0)),
        out_shape=jax.ShapeDtypeStruct((s, d), BF16),
        compiler_params=pltpu.CompilerParams(dimension_semantics=("arbitrary",)),
        name="norm",
    )(x, g.reshape(1, d))


def _matmul_kernel(h_ref, w_ref, o_ref, *, q_tiles, q_scale):
    acc = jnp.dot(h_ref[...], w_ref[...].astype(h_ref.dtype), preferred_element_type=F32)
    if q_tiles:
        acc = acc * jnp.where(pl.program_id(1) < q_tiles, q_scale, 1.0).astype(F32)
    o_ref[...] = acc.astype(o_ref.dtype)


def _matmul(h, w, layer, *, q_scale=None, tm=1024, tn=512):
    s, d = h.shape
    n = w.shape[2]
    q_tiles = (n // 2) // tn if q_scale is not None else 0
    kern = functools.partial(_matmul_kernel, q_tiles=q_tiles, q_scale=q_scale if q_scale is not None else 1.0)
    vmem = 2 * tm * d * 2 + 2 * d * tn * w.dtype.itemsize + d * tn * 2 + 2 * tm * tn * 2 + tm * tn * 4
    return pl.pallas_call(
        kern,
        grid=(s // tm, n // tn),
        in_specs=[
            pl.BlockSpec((tm, d), lambda i, j: (i, 0)),
            pl.BlockSpec((None, d, tn), lambda i, j: (layer, 0, j)),
        ],
        out_specs=pl.BlockSpec((tm, tn), lambda i, j: (i, j)),
        out_shape=jax.ShapeDtypeStruct((s, n), BF16),
        compiler_params=pltpu.CompilerParams(
            dimension_semantics=("arbitrary", "arbitrary"),
            vmem_limit_bytes=_vmem_limit(vmem)),
        name="in_proj",
    )(h, w)


POOL_SUB = 128


def _pool_kernel(u_ref, halo_ref, gate_ref, wg_ref, scale_ref, o_ref, pooled_ref):
    i = pl.program_id(0)
    g = pl.program_id(1)
    tm = u_ref.shape[0]
    w = lax.shift_left(jnp.int32(2), g)

    r = lax.broadcasted_iota(jnp.int32, (POOL_SUB, POOL_SUB), 0)
    c = lax.broadcasted_iota(jnp.int32, (POOL_SUB, POOL_SUB), 1)
    d_cur = r - c
    d_prev = d_cur + POOL_SUB
    band_cur = ((d_cur >= 0) & (d_cur < w)).astype(BF16)
    band_prev = (d_prev < w).astype(BF16)

    for rb in range(tm // POOL_SUB):
        rows = slice(rb * POOL_SUB, (rb + 1) * POOL_SUB)
        cur = u_ref[rows, :]
        if rb == 0:
            halo = halo_ref[...]
            prev = jnp.where(i > 0, halo, jnp.zeros_like(halo))
        else:
            prev = u_ref[(rb - 1) * POOL_SUB:rb * POOL_SUB, :]
        wsum = (jnp.dot(band_cur, cur, preferred_element_type=F32)
                + jnp.dot(band_prev, prev, preferred_element_type=F32))
        t = i * tm + rb * POOL_SUB + lax.broadcasted_iota(jnp.int32, (POOL_SUB, 1), 0)
        inv_count = 1.0 / jnp.minimum(t + 1, w).astype(F32)
        pooled_ref[rows, :] = (wsum * inv_count - cur.astype(F32)).astype(pooled_ref.dtype)

    mixed = jnp.dot(pooled_ref[...], wg_ref[...].astype(BF16), preferred_element_type=F32)
    gate = gate_ref[...].astype(F32)
    o_ref[...] = (mixed * scale_ref[...] * (gate * jax.nn.sigmoid(gate))).astype(o_ref.dtype)


def _pool_mix(ug, w_group, scale, layer, *, tm=512):
    s = ug.shape[0]
    _, n_groups, gd, _ = w_group.shape
    width = n_groups * gd
    sub_per_tile = tm // POOL_SUB
    vmem = (2 * (2 * tm * gd * 2 + POOL_SUB * gd * 2 + gd * gd * w_group.dtype.itemsize + tm * gd * 2)
            + tm * gd * 2 + gd * gd * 2 + 2 * tm * gd * 4)
    return pl.pallas_call(
        _pool_kernel,
        grid=(s // tm, n_groups),
        in_specs=[
            pl.BlockSpec((tm, gd), lambda i, g: (i, g)),
            pl.BlockSpec((POOL_SUB, gd), lambda i, g: (jnp.maximum(i * sub_per_tile - 1, 0), g)),
            pl.BlockSpec((tm, gd), lambda i, g: (i, n_groups + g)),
            pl.BlockSpec((None, None, gd, gd), lambda i, g: (layer, g, 0, 0)),
            pl.BlockSpec((1, gd), lambda i, g: (0, g)),
        ],
        out_specs=pl.BlockSpec((tm, gd), lambda i, g: (i, g)),
        out_shape=jax.ShapeDtypeStruct((s, width), BF16),
        scratch_shapes=[pltpu.VMEM((tm, gd), BF16)],
        compiler_params=pltpu.CompilerParams(
            dimension_semantics=("arbitrary", "arbitrary"),
            vmem_limit_bytes=_vmem_limit(vmem)),
        name="pool_mix",
    )(ug, ug, ug, w_group, scale.reshape(1, width))


def _out_proj_kernel(a_ref, w_ref, x_ref, gpost_ref, gnext_ref, xo_ref, *maybe_ho_ref, tn):
    j = pl.program_id(1)
    tm = xo_ref.shape[0]
    cols = pl.ds(pl.multiple_of(j * tn, tn), tn)
    xo_ref[:, cols] = jnp.dot(a_ref[...], w_ref[...], preferred_element_type=F32)

    @pl.when(j == pl.num_programs(1) - 1)
    def _():
        gpost = gpost_ref[...]
        gnext = gnext_ref[...]

        def body(c, _):
            r = pl.ds(pl.multiple_of(c * NORM_ROW_CHUNK, NORM_ROW_CHUNK), NORM_ROW_CHUNK)
            y = xo_ref[r, :]
            xn = x_ref[r, :] + y * _rms_scale(y) * gpost
            xo_ref[r, :] = xn
            for ho_ref in maybe_ho_ref:
                ho_ref[r, :] = (xn * _rms_scale(xn) * gnext).astype(ho_ref.dtype)
            return 0

        lax.fori_loop(0, tm // NORM_ROW_CHUNK, body, 0)


def _out_proj(a, w, layer, x, g_post, g_next=None, *, tm=512, tn=512):
    s, kdim = a.shape
    d = w.shape[2]
    want_h = g_next is not None
    kern = functools.partial(_out_proj_kernel, tn=tn)
    out_shape = [jax.ShapeDtypeStruct((s, d), F32)]
    out_specs = [pl.BlockSpec((tm, d), lambda i, j: (i, 0))]
    if want_h:
        out_shape.append(jax.ShapeDtypeStruct((s, d), BF16))
        out_specs.append(pl.BlockSpec((tm, d), lambda i, j: (i, 0)))
    vmem = (2 * tm * kdim * 2 + 2 * kdim * tn * 2 + tm * d * 4 + 2 * tm * d * 4
            + (2 * tm * d * 2 if want_h else 0) + tm * tn * 4)
    outs = pl.pallas_call(
        kern,
        grid=(s // tm, d // tn),
        in_specs=[
            pl.BlockSpec((tm, kdim), lambda i, j: (i, 0)),
            pl.BlockSpec((None, kdim, tn), lambda i, j: (layer, 0, j)),
            pl.BlockSpec((tm, d), lambda i, j: (i, 0), pipeline_mode=pl.Buffered(1)),
            pl.BlockSpec((1, d), lambda i, j: (0, 0)),
            pl.BlockSpec((1, d), lambda i, j: (0, 0)),
        ],
        out_specs=out_specs,
        out_shape=out_shape,
        compiler_params=pltpu.CompilerParams(
            dimension_semantics=("arbitrary", "arbitrary"),
            vmem_limit_bytes=_vmem_limit(vmem)),
        name="out_proj",
    )(a, w, x, g_post.reshape(1, d), (g_next if want_h else g_post).reshape(1, d))
    return (outs[0], outs[1]) if want_h else (outs[0], None)


LOG2_E = math.log2(math.e)
SOFTPLUS2_LINEAR_ABOVE = 64.0
MASKED_EXPONENT = -1e30
SKIP_CARRY = 160.0


def _softplus2(z):
    return jnp.where(z > SOFTPLUS2_LINEAR_ABOVE, z, jnp.log(1.0 + jnp.exp2(z)) * LOG2_E)


def _sb_attn_kernel(q_ref, k_ref, v_ref, gate_ref, o_ref, acc_ref, carry_ref, *, tq, bk):
    i = pl.program_id(1)
    top = 2 * i + 1
    r = lax.broadcasted_iota(jnp.int32, (bk, bk), 0)
    c = lax.broadcasted_iota(jnp.int32, (bk, bk), 1)
    suffix = (r >= c).astype(BF16)

    def key_rows(kb):
        return pl.ds(pl.multiple_of(kb * bk, bk), bk)

    def scores(q_rows, kb, masked):
        krows = key_rows(kb)
        z = lax.dot_general(q_ref[q_rows, :], k_ref[krows, :], (((1,), (1,)), ((), ())),
                            preferred_element_type=F32)
        sp = _softplus2(z)
        if masked:
            t_pos = i * tq + q_rows.start + lax.broadcasted_iota(jnp.int32, z.shape, 0)
            s_pos = krows.start + lax.broadcasted_iota(jnp.int32, z.shape, 1)
            causal = s_pos < t_pos
            sp = jnp.where(causal, sp, 0.0)
        s_in = jnp.dot(sp.astype(BF16), suffix, preferred_element_type=F32)
        expo = jnp.minimum(z - s_in, 0.0)
        if masked:
            expo = jnp.where(causal, expo, MASKED_EXPONENT)
        return expo, s_in[:, 0:1]

    def accumulate(q_rows, kb, expo, tot):
        carry = carry_ref[q_rows, :]
        a = jnp.exp2(expo - carry)
        acc_ref[q_rows, :] += jnp.dot(a.astype(BF16), v_ref[key_rows(kb), :], preferred_element_type=F32)
        carry_ref[q_rows, :] = carry + tot

    def run_blocks(blocks):
        parts = [scores(*blk) for blk in blocks]
        for (q_rows, kb, _), (expo, tot) in zip(blocks, parts):
            accumulate(q_rows, kb, expo, tot)

    acc_ref[...] = jnp.zeros_like(acc_ref)
    carry_ref[...] = jnp.zeros_like(carry_ref)
    all_rows = pl.ds(0, tq)
    late_rows = pl.ds(bk, tq - bk)

    @pl.when(i == 0)
    def _():
        run_blocks([(late_rows, top, True), (all_rows, top - 1, True)])

    @pl.when(i > 0)
    def _():
        run_blocks([(late_rows, top, True), (all_rows, top - 1, True), (all_rows, top - 2, False)])

    def keep_going(state):
        kb, min_carry = state
        return (kb >= 0) & (min_carry < SKIP_CARRY)

    def body(state):
        kb, _ = state
        run_blocks([(all_rows, kb, False)])
        return kb - 1, jnp.min(carry_ref[...])

    lax.while_loop(keep_going, body, (top - 3, jnp.min(carry_ref[...])))

    gate = gate_ref[...].astype(F32)
    o_ref[...] = (acc_ref[...] * (gate * jax.nn.sigmoid(gate))).astype(o_ref.dtype)


def _sb_attention(qg, kv, *, bk=256):
    s, two_w = qg.shape
    n_heads = (two_w // 2) // HEAD_DIM
    tq = 2 * bk
    kern = functools.partial(_sb_attn_kernel, tq=tq, bk=bk)
    vmem = 2 * (2 * s * HEAD_DIM * 2 + 3 * tq * HEAD_DIM * 2) + 16 * tq * bk * 4
    return pl.pallas_call(
        kern,
        grid=(n_heads, s // tq),
        in_specs=[
            pl.BlockSpec((tq, HEAD_DIM), lambda h, i: (i, h)),
            pl.BlockSpec((s, HEAD_DIM), lambda h, i: (0, h)),
            pl.BlockSpec((s, HEAD_DIM), lambda h, i: (0, n_heads + h)),
            pl.BlockSpec((tq, HEAD_DIM), lambda h, i: (i, n_heads + h)),
        ],
        out_specs=pl.BlockSpec((tq, HEAD_DIM), lambda h, i: (i, h)),
        out_shape=jax.ShapeDtypeStruct((s, n_heads * HEAD_DIM), BF16),
        scratch_shapes=[pltpu.VMEM((tq, HEAD_DIM), F32), pltpu.VMEM((tq, 1), F32)],
        compiler_params=pltpu.CompilerParams(
            dimension_semantics=("arbitrary", "arbitrary"),
            vmem_limit_bytes=_vmem_limit(vmem)),
        name="sb_attention",
    )(qg, kv, kv, qg)


def kernel(x, a_pre_norm, a_w_in, a_w_group, a_scale, a_w_out, a_post_norm, kv_norm, w_kv,
           b_pre_norm, b_w_in, b_w_out, b_post_norm):
    b, s, d = x.shape
    assert b == 1
    n_a, n_b = a_w_in.shape[0], b_w_in.shape[0]
    a_w_out, b_w_out = a_w_out.astype(BF16), b_w_out.astype(BF16)
    w_kv = w_kv[None]

    xs = x.reshape(s, d)
    h = _norm(xs, a_pre_norm[0])
    for layer in range(n_a):
        ug = _matmul(h, a_w_in, layer)
        mixed = _pool_mix(ug, a_w_group, a_scale[layer], layer)
        g_next = a_pre_norm[layer + 1] if layer + 1 < n_a else b_pre_norm[0]
        xs, h = _out_proj(mixed, a_w_out, layer, xs, a_post_norm[layer], g_next)
    kv = _matmul(_norm(xs, kv_norm), w_kv, 0)
    for layer in range(n_b):
        qg = _matmul(h, b_w_in, layer, q_scale=LOG2_E / math.sqrt(HEAD_DIM))
        og = _sb_attention(qg, kv)
        g_next = b_pre_norm[layer + 1] if layer + 1 < n_b else None
        xs, h = _out_proj(og, b_w_out, layer, xs, b_post_norm[layer], g_next)
    return xs.reshape(b, s, d)
```

```python
import functools
import math

import jax
import jax.numpy as jnp
from jax import lax
from jax.experimental import pallas as pl
from jax.experimental.pallas import tpu as pltpu

RMS_EPS = 1e-6
HEAD_DIM = 128

V7X_VMEM_BYTES = 64 * 1024 * 1024

BF16 = jnp.bfloat16
F32 = jnp.float32
NORM_ROW_CHUNK = 32


def _vmem_limit(nbytes):
    return int(min(V7X_VMEM_BYTES - 6 * 1024 * 1024, nbytes + 8 * 1024 * 1024))


def _rms_scale(x):
    return lax.rsqrt(jnp.mean(x * x, axis=-1, keepdims=True) + RMS_EPS)


def _norm_kernel(x_ref, g_ref, o_ref):
    x = x_ref[...]
    o_ref[...] = (x * _rms_scale(x) * g_ref[...]).astype(o_ref.dtype)


def _norm(x, g, *, tm=256):
    s, d = x.shape
    return pl.pallas_call(
        _norm_kernel,
        grid=(s // tm,),
        in_specs=[pl.BlockSpec((tm, d), lambda i: (i, 0)), pl.BlockSpec((1, d), lambda i: (0, 0))],
        out_specs=pl.BlockSpec((tm, d), lambda i: (i, 0)),
        out_shape=jax.ShapeDtypeStruct((s, d), BF16),
        compiler_params=pltpu.CompilerParams(dimension_semantics=("arbitrary",)),
        name="norm",
    )(x, g.reshape(1, d))


def _matmul_kernel(h_ref, w_ref, o_ref, *, q_tiles, q_scale):
    acc = jnp.dot(h_ref[...], w_ref[...].astype(h_ref.dtype), preferred_element_type=F32)
    if q_tiles:
        acc = acc * jnp.where(pl.program_id(1) < q_tiles, q_scale, 1.0).astype(F32)
    o_ref[...] = acc.astype(o_ref.dtype)


def _matmul(h, w, layer, *, q_scale=None, tm=1024, tn=512):
    s, d = h.shape
    n = w.shape[2]
    q_tiles = (n // 2) // tn if q_scale is not None else 0
    kern = functools.partial(_matmul_kernel, q_tiles=q_tiles, q_scale=q_scale if q_scale is not None else 1.0)
    vmem = 2 * tm * d * 2 + 2 * d * tn * w.dtype.itemsize + d * tn * 2 + 2 * tm * tn * 2 + tm * tn * 4
    return pl.pallas_call(
        kern,
        grid=(s // tm, n // tn),
        in_specs=[
            pl.BlockSpec((tm, d), lambda i, j: (i, 0)),
            pl.BlockSpec((None, d, tn), lambda i, j: (layer, 0, j)),
        ],
        out_specs=pl.BlockSpec((tm, tn), lambda i, j: (i, j)),
        out_shape=jax.ShapeDtypeStruct((s, n), BF16),
        compiler_params=pltpu.CompilerParams(
            dimension_semantics=("arbitrary", "arbitrary"),
            vmem_limit_bytes=_vmem_limit(vmem)),
        name="in_proj",
    )(h, w)


POOL_SUB = 128


def _pool_kernel(u_ref, halo_ref, gate_ref, wg_ref, scale_ref, o_ref, pooled_ref):
    g = pl.program_id(0)
    i = pl.program_id(1)
    tm = u_ref.shape[0]
    w = lax.shift_left(jnp.int32(2), g)

    r = lax.broadcasted_iota(jnp.int32, (POOL_SUB, POOL_SUB), 0)
    c = lax.broadcasted_iota(jnp.int32, (POOL_SUB, POOL_SUB), 1)
    d_cur = r - c
    d_prev = d_cur + POOL_SUB
    band_cur = ((d_cur >= 0) & (d_cur < w)).astype(BF16)
    band_prev = (d_prev < w).astype(BF16)

    for rb in range(tm // POOL_SUB):
        rows = slice(rb * POOL_SUB, (rb + 1) * POOL_SUB)
        cur = u_ref[rows, :]
        if rb == 0:
            halo = halo_ref[...]
            prev = jnp.where(i > 0, halo, jnp.zeros_like(halo))
        else:
            prev = u_ref[(rb - 1) * POOL_SUB:rb * POOL_SUB, :]
        wsum = (jnp.dot(band_cur, cur, preferred_element_type=F32)
                + jnp.dot(band_prev, prev, preferred_element_type=F32))
        t = i * tm + rb * POOL_SUB + lax.broadcasted_iota(jnp.int32, (POOL_SUB, 1), 0)
        inv_count = 1.0 / jnp.minimum(t + 1, w).astype(F32)
        pooled_ref[rows, :] = (wsum * inv_count - cur.astype(F32)).astype(pooled_ref.dtype)

    mixed = jnp.dot(pooled_ref[...], wg_ref[...].astype(BF16), preferred_element_type=F32)
    gate = gate_ref[...].astype(F32)
    o_ref[...] = (mixed * scale_ref[...] * (gate * jax.nn.sigmoid(gate))).astype(o_ref.dtype)


def _pool_mix(ug, w_group, scale, layer, *, tm=512):
    s = ug.shape[0]
    _, n_groups, gd, _ = w_group.shape
    width = n_groups * gd
    sub_per_tile = tm // POOL_SUB
    vmem = (2 * (2 * tm * gd * 2 + POOL_SUB * gd * 2 + gd * gd * w_group.dtype.itemsize + tm * gd * 2)
            + tm * gd * 2 + gd * gd * 2 + 2 * tm * gd * 4)
    return pl.pallas_call(
        _pool_kernel,
        grid=(n_groups, s // tm),
        in_specs=[
            pl.BlockSpec((tm, gd), lambda g, i: (i, g)),
            pl.BlockSpec((POOL_SUB, gd), lambda g, i: (jnp.maximum(i * sub_per_tile - 1, 0), g)),
            pl.BlockSpec((tm, gd), lambda g, i: (i, n_groups + g)),
            pl.BlockSpec((None, None, gd, gd), lambda g, i: (layer, g, 0, 0)),
            pl.BlockSpec((1, gd), lambda g, i: (0, g)),
        ],
        out_specs=pl.BlockSpec((tm, gd), lambda g, i: (i, g)),
        out_shape=jax.ShapeDtypeStruct((s, width), BF16),
        scratch_shapes=[pltpu.VMEM((tm, gd), BF16)],
        compiler_params=pltpu.CompilerParams(
            dimension_semantics=("arbitrary", "arbitrary"),
            vmem_limit_bytes=_vmem_limit(vmem)),
        name="pool_mix",
    )(ug, ug, ug, w_group, scale.reshape(1, width))


def _out_proj_kernel(a_ref, w_ref, x_ref, gpost_ref, gnext_ref, xo_ref, *rest, tn):
    *maybe_ho_ref, xfull_ref = rest
    j = pl.program_id(1)
    tm = xo_ref.shape[0]
    cols = pl.ds(pl.multiple_of(j * tn, tn), tn)
    xo_ref[:, cols] = jnp.dot(a_ref[...], w_ref[...], preferred_element_type=F32)
    xfull_ref[:, cols] = x_ref[...]

    @pl.when(j == pl.num_programs(1) - 1)
    def _():
        gpost = gpost_ref[...]
        gnext = gnext_ref[...]

        def body(c, _):
            r = pl.ds(pl.multiple_of(c * NORM_ROW_CHUNK, NORM_ROW_CHUNK), NORM_ROW_CHUNK)
            y = xo_ref[r, :]
            xn = xfull_ref[r, :] + y * _rms_scale(y) * gpost
            xo_ref[r, :] = xn
            for ho_ref in maybe_ho_ref:
                ho_ref[r, :] = (xn * _rms_scale(xn) * gnext).astype(ho_ref.dtype)
            return 0

        lax.fori_loop(0, tm // NORM_ROW_CHUNK, body, 0)


def _out_proj(a, w, layer, x, g_post, g_next=None, *, tm=512, tn=512):
    s, kdim = a.shape
    d = w.shape[2]
    want_h = g_next is not None
    kern = functools.partial(_out_proj_kernel, tn=tn)
    out_shape = [jax.ShapeDtypeStruct((s, d), F32)]
    out_specs = [pl.BlockSpec((tm, d), lambda i, j: (i, 0))]
    if want_h:
        out_shape.append(jax.ShapeDtypeStruct((s, d), BF16))
        out_specs.append(pl.BlockSpec((tm, d), lambda i, j: (i, 0)))
    vmem = (2 * tm * kdim * 2 + 2 * kdim * tn * 2 + 2 * tm * tn * 4 + tm * d * 4 + 2 * tm * d * 4
            + (2 * tm * d * 2 if want_h else 0) + tm * tn * 4)
    outs = pl.pallas_call(
        kern,
        grid=(s // tm, d // tn),
        in_specs=[
            pl.BlockSpec((tm, kdim), lambda i, j: (i, 0)),
            pl.BlockSpec((None, kdim, tn), lambda i, j: (layer, 0, j)),
            pl.BlockSpec((tm, tn), lambda i, j: (i, j)),
            pl.BlockSpec((1, d), lambda i, j: (0, 0)),
            pl.BlockSpec((1, d), lambda i, j: (0, 0)),
        ],
        out_specs=out_specs,
        out_shape=out_shape,
        scratch_shapes=[pltpu.VMEM((tm, d), F32)],
        compiler_params=pltpu.CompilerParams(
            dimension_semantics=("arbitrary", "arbitrary"),
            vmem_limit_bytes=_vmem_limit(vmem)),
        name="out_proj",
    )(a, w, x, g_post.reshape(1, d), (g_next if want_h else g_post).reshape(1, d))
    return (outs[0], outs[1]) if want_h else (outs[0], None)


LOG2_E = math.log2(math.e)
SOFTPLUS2_LINEAR_ABOVE = 64.0
MASKED_EXPONENT = -1e30
SKIP_CARRY = 160.0


def _softplus2(z):
    return jnp.where(z > SOFTPLUS2_LINEAR_ABOVE, z, jnp.log(1.0 + jnp.exp2(z)) * LOG2_E)


def _sb_attn_kernel(q_ref, k_ref, v_ref, gate_ref, o_ref, acc_ref, carry_ref, *, tq, bk, n_heads):
    i = pl.program_id(1)
    top = 2 * i + 1
    r = lax.broadcasted_iota(jnp.int32, (bk, bk), 0)
    c = lax.broadcasted_iota(jnp.int32, (bk, bk), 1)
    suffix = (r >= c).astype(BF16)
    before = c < r

    def key_rows(kb):
        return pl.ds(pl.multiple_of(kb * bk, bk), bk)

    def head_cols(head):
        return pl.ds(head * HEAD_DIM, HEAD_DIM)

    def scores(head, q_rows, kb, diagonal):
        z = lax.dot_general(q_ref[q_rows, head_cols(head)], k_ref[key_rows(kb), head_cols(head)],
                            (((1,), (1,)), ((), ())), preferred_element_type=F32)
        sp = _softplus2(z)
        if diagonal:
            sp = jnp.where(before, sp, 0.0)
        s_in = jnp.dot(sp.astype(BF16), suffix, preferred_element_type=F32)
        expo = jnp.minimum(z - s_in, 0.0)
        if diagonal:
            expo = jnp.where(before, expo, MASKED_EXPONENT)
        return expo, s_in[:, 0:1]

    def accumulate(head, q_rows, kb, expo, tot):
        carry = carry_ref[head, q_rows, :]
        a = jnp.exp2(expo - carry)
        acc_ref[q_rows, head_cols(head)] += jnp.dot(a.astype(BF16), v_ref[key_rows(kb), head_cols(head)],
                                                    preferred_element_type=F32)
        carry_ref[head, q_rows, :] = carry + tot

    def run_blocks(blocks):
        work = [(head,) + blk for blk in blocks for head in range(n_heads)]
        parts = [scores(*item) for item in work]
        for (head, q_rows, kb, _), (expo, tot) in zip(work, parts):
            accumulate(head, q_rows, kb, expo, tot)

    def min_carry():
        return jnp.min(carry_ref[...])

    acc_ref[...] = jnp.zeros_like(acc_ref)
    carry_ref[...] = jnp.zeros_like(carry_ref)
    early_rows = pl.ds(0, bk)
    late_rows = pl.ds(bk, bk)
    all_rows = pl.ds(0, tq)

    @pl.when(i == 0)
    def _():
        run_blocks([(late_rows, top, True), (early_rows, top - 1, True), (late_rows, top - 1, False)])

    @pl.when(i > 0)
    def _():
        run_blocks([(late_rows, top, True), (early_rows, top - 1, True),
                    (late_rows, top - 1, False), (early_rows, top - 2, False)])

    @pl.when((i > 0) & (min_carry() < SKIP_CARRY))
    def _():
        run_blocks([(late_rows, top - 2, False)])

    def keep_going(state):
        kb, lowest = state
        return (kb >= 0) & (lowest < SKIP_CARRY)

    def body(state):
        kb, _ = state
        run_blocks([(all_rows, kb, False)])
        return kb - 1, min_carry()

    lax.while_loop(keep_going, body, (top - 3, min_carry()))

    gate = gate_ref[...].astype(F32)
    o_ref[...] = (acc_ref[...] * (gate * jax.nn.sigmoid(gate))).astype(o_ref.dtype)


def _sb_attention(qg, kv, *, bk=256, heads_per_step=2):
    s, two_w = qg.shape
    n_groups = (two_w // 2) // (heads_per_step * HEAD_DIM)
    tq = 2 * bk
    gw = heads_per_step * HEAD_DIM
    kern = functools.partial(_sb_attn_kernel, tq=tq, bk=bk, n_heads=heads_per_step)
    vmem = 2 * (2 * s * gw * 2 + 3 * tq * gw * 2) + tq * gw * 4 + heads_per_step * 16 * tq * bk * 4
    return pl.pallas_call(
        kern,
        grid=(n_groups, s // tq),
        in_specs=[
            pl.BlockSpec((tq, gw), lambda h, i: (i, h)),
            pl.BlockSpec((s, gw), lambda h, i: (0, h)),
            pl.BlockSpec((s, gw), lambda h, i: (0, n_groups + h)),
            pl.BlockSpec((tq, gw), lambda h, i: (i, n_groups + h)),
        ],
        out_specs=pl.BlockSpec((tq, gw), lambda h, i: (i, h)),
        out_shape=jax.ShapeDtypeStruct((s, n_groups * gw), BF16),
        scratch_shapes=[pltpu.VMEM((tq, gw), F32), pltpu.VMEM((heads_per_step, tq, 1), F32)],
        compiler_params=pltpu.CompilerParams(
            dimension_semantics=("arbitrary", "arbitrary"),
            vmem_limit_bytes=_vmem_limit(vmem)),
        name="sb_attention",
    )(qg, kv, kv, qg)


def kernel(x, a_pre_norm, a_w_in, a_w_group, a_scale, a_w_out, a_post_norm, kv_norm, w_kv,
           b_pre_norm, b_w_in, b_w_out, b_post_norm):
    b, s, d = x.shape
    assert b == 1
    n_a, n_b = a_w_in.shape[0], b_w_in.shape[0]
    a_w_out, b_w_out = a_w_out.astype(BF16), b_w_out.astype(BF16)
    w_kv = w_kv[None]

    xs = x.reshape(s, d)
    h = _norm(xs, a_pre_norm[0])
    for layer in range(n_a):
        ug = _matmul(h, a_w_in, layer)
        mixed = _pool_mix(ug, a_w_group, a_scale[layer], layer)
        g_next = a_pre_norm[layer + 1] if layer + 1 < n_a else b_pre_norm[0]
        xs, h = _out_proj(mixed, a_w_out, layer, xs, a_post_norm[layer], g_next)
    kv = _matmul(_norm(xs, kv_norm), w_kv, 0)
    for layer in range(n_b):
        qg = _matmul(h, b_w_in, layer, q_scale=LOG2_E / math.sqrt(HEAD_DIM))
        og = _sb_attention(qg, kv)
        g_next = b_pre_norm[layer + 1] if layer + 1 < n_b else None
        xs, h = _out_proj(og, b_w_out, layer, xs, b_post_norm[layer], g_next)
    return xs.reshape(b, s, d)
```

```python
import functools
import math

import jax
import jax.numpy as jnp
from jax import lax
from jax.experimental import pallas as pl
from jax.experimental.pallas import tpu as pltpu

RMS_EPS = 1e-6
HEAD_DIM = 128

V7X_VMEM_BYTES = 64 * 1024 * 1024

BF16 = jnp.bfloat16
F32 = jnp.float32
NORM_ROW_CHUNK = 32


def _vmem_limit(nbytes):
    return int(min(V7X_VMEM_BYTES - 6 * 1024 * 1024, nbytes + 8 * 1024 * 1024))


def _rms_scale(x):
    return lax.rsqrt(jnp.mean(x * x, axis=-1, keepdims=True) + RMS_EPS)


def _norm_kernel(x_ref, g_ref, o_ref):
    x = x_ref[...]
    o_ref[...] = (x * _rms_scale(x) * g_ref[...]).astype(o_ref.dtype)


def _norm(x, g, *, tm=256):
    s, d = x.shape
    return pl.pallas_call(
        _norm_kernel,
        grid=(s // tm,),
        in_specs=[pl.BlockSpec((tm, d), lambda i: (i, 0)), pl.BlockSpec((1, d), lambda i: (0, 0))],
        out_specs=pl.BlockSpec((tm, d), lambda i: (i, 0)),
        out_shape=jax.ShapeDtypeStruct((s, d), BF16),
        compiler_params=pltpu.CompilerParams(dimension_semantics=("arbitrary",)),
        name="norm",
    )(x, g.reshape(1, d))


def _matmul_kernel(h_ref, w_ref, o_ref, *, q_tiles, q_scale):
    acc = jnp.dot(h_ref[...], w_ref[...].astype(h_ref.dtype), preferred_element_type=F32)
    if q_tiles:
        acc = acc * jnp.where(pl.program_id(1) < q_tiles, q_scale, 1.0).astype(F32)
    o_ref[...] = acc.astype(o_ref.dtype)


def _matmul(h, w, layer, *, q_scale=None, tm=1024, tn=512):
    s, d = h.shape
    n = w.shape[2]
    q_tiles = (n // 2) // tn if q_scale is not None else 0
    kern = functools.partial(_matmul_kernel, q_tiles=q_tiles, q_scale=q_scale if q_scale is not None else 1.0)
    vmem = 2 * tm * d * 2 + 2 * d * tn * w.dtype.itemsize + d * tn * 2 + 2 * tm * tn * 2 + tm * tn * 4
    return pl.pallas_call(
        kern,
        grid=(s // tm, n // tn),
        in_specs=[
            pl.BlockSpec((tm, d), lambda i, j: (i, 0)),
            pl.BlockSpec((None, d, tn), lambda i, j: (layer, 0, j)),
        ],
        out_specs=pl.BlockSpec((tm, tn), lambda i, j: (i, j)),
        out_shape=jax.ShapeDtypeStruct((s, n), BF16),
        compiler_params=pltpu.CompilerParams(
            dimension_semantics=("arbitrary", "arbitrary"),
            vmem_limit_bytes=_vmem_limit(vmem)),
        name="in_proj",
    )(h, w)


POOL_SUB = 128


def _pool_kernel(u_ref, halo_ref, gate_ref, wg_ref, scale_ref, o_ref, pooled_ref):
    g = pl.program_id(0)
    i = pl.program_id(1)
    tm = u_ref.shape[0]
    w = lax.shift_left(jnp.int32(2), g)

    r = lax.broadcasted_iota(jnp.int32, (POOL_SUB, POOL_SUB), 0)
    c = lax.broadcasted_iota(jnp.int32, (POOL_SUB, POOL_SUB), 1)
    d_cur = r - c
    d_prev = d_cur + POOL_SUB
    band_cur = ((d_cur >= 0) & (d_cur < w)).astype(BF16)
    band_prev = (d_prev < w).astype(BF16)

    for rb in range(tm // POOL_SUB):
        rows = slice(rb * POOL_SUB, (rb + 1) * POOL_SUB)
        cur = u_ref[rows, :]
        if rb == 0:
            halo = halo_ref[...]
            prev = jnp.where(i > 0, halo, jnp.zeros_like(halo))
        else:
            prev = u_ref[(rb - 1) * POOL_SUB:rb * POOL_SUB, :]
        wsum = (jnp.dot(band_cur, cur, preferred_element_type=F32)
                + jnp.dot(band_prev, prev, preferred_element_type=F32))
        t = i * tm + rb * POOL_SUB + lax.broadcasted_iota(jnp.int32, (POOL_SUB, 1), 0)
        inv_count = 1.0 / jnp.minimum(t + 1, w).astype(F32)
        pooled_ref[rows, :] = (wsum * inv_count - cur.astype(F32)).astype(pooled_ref.dtype)

    mixed = jnp.dot(pooled_ref[...], wg_ref[...].astype(BF16), preferred_element_type=F32)
    gate = gate_ref[...].astype(F32)
    o_ref[...] = (mixed * scale_ref[...] * (gate * jax.nn.sigmoid(gate))).astype(o_ref.dtype)


def _pool_mix(ug, w_group, scale, layer, *, tm=512):
    s = ug.shape[0]
    _, n_groups, gd, _ = w_group.shape
    width = n_groups * gd
    sub_per_tile = tm // POOL_SUB
    vmem = (2 * (2 * tm * gd * 2 + POOL_SUB * gd * 2 + gd * gd * w_group.dtype.itemsize + tm * gd * 2)
            + tm * gd * 2 + gd * gd * 2 + 2 * tm * gd * 4)
    return pl.pallas_call(
        _pool_kernel,
        grid=(n_groups, s // tm),
        in_specs=[
            pl.BlockSpec((tm, gd), lambda g, i: (i, g)),
            pl.BlockSpec((POOL_SUB, gd), lambda g, i: (jnp.maximum(i * sub_per_tile - 1, 0), g)),
            pl.BlockSpec((tm, gd), lambda g, i: (i, n_groups + g)),
            pl.BlockSpec((None, None, gd, gd), lambda g, i: (layer, g, 0, 0)),
            pl.BlockSpec((1, gd), lambda g, i: (0, g)),
        ],
        out_specs=pl.BlockSpec((tm, gd), lambda g, i: (i, g)),
        out_shape=jax.ShapeDtypeStruct((s, width), BF16),
        scratch_shapes=[pltpu.VMEM((tm, gd), BF16)],
        compiler_params=pltpu.CompilerParams(
            dimension_semantics=("arbitrary", "arbitrary"),
            vmem_limit_bytes=_vmem_limit(vmem)),
        name="pool_mix",
    )(ug, ug, ug, w_group, scale.reshape(1, width))


def _out_proj_kernel(a_ref, w_ref, x_ref, gpost_ref, gnext_ref, xo_ref, *rest, tn):
    *maybe_ho_ref, xfull_ref = rest
    j = pl.program_id(1)
    tm = xo_ref.shape[0]
    cols = pl.ds(pl.multiple_of(j * tn, tn), tn)
    xo_ref[:, cols] = jnp.dot(a_ref[...], w_ref[...], preferred_element_type=F32)
    xfull_ref[:, cols] = x_ref[...]

    @pl.when(j == pl.num_programs(1) - 1)
    def _():
        gpost = gpost_ref[...]
        gnext = gnext_ref[...]

        for c in range(tm // NORM_ROW_CHUNK):
            r = pl.ds(c * NORM_ROW_CHUNK, NORM_ROW_CHUNK)
            y = xo_ref[r, :]
            xn = xfull_ref[r, :] + y * _rms_scale(y) * gpost
            xo_ref[r, :] = xn
            for ho_ref in maybe_ho_ref:
                ho_ref[r, :] = (xn * _rms_scale(xn) * gnext).astype(ho_ref.dtype)


def _out_proj(a, w, layer, x, g_post, g_next=None, *, tm=512, tn=512):
    s, kdim = a.shape
    d = w.shape[2]
    want_h = g_next is not None
    kern = functools.partial(_out_proj_kernel, tn=tn)
    out_shape = [jax.ShapeDtypeStruct((s, d), F32)]
    out_specs = [pl.BlockSpec((tm, d), lambda i, j: (i, 0))]
    if want_h:
        out_shape.append(jax.ShapeDtypeStruct((s, d), BF16))
        out_specs.append(pl.BlockSpec((tm, d), lambda i, j: (i, 0)))
    vmem = (2 * tm * kdim * 2 + 2 * kdim * tn * 2 + 2 * tm * tn * 4 + tm * d * 4 + 2 * tm * d * 4
            + (2 * tm * d * 2 if want_h else 0) + tm * tn * 4)
    outs = pl.pallas_call(
        kern,
        grid=(s // tm, d // tn),
        in_specs=[
            pl.BlockSpec((tm, kdim), lambda i, j: (i, 0)),
            pl.BlockSpec((None, kdim, tn), lambda i, j: (layer, 0, j)),
            pl.BlockSpec((tm, tn), lambda i, j: (i, j)),
            pl.BlockSpec((1, d), lambda i, j: (0, 0)),
            pl.BlockSpec((1, d), lambda i, j: (0, 0)),
        ],
        out_specs=out_specs,
        out_shape=out_shape,
        scratch_shapes=[pltpu.VMEM((tm, d), F32)],
        compiler_params=pltpu.CompilerParams(
            dimension_semantics=("arbitrary", "arbitrary"),
            vmem_limit_bytes=_vmem_limit(vmem)),
        name="out_proj",
    )(a, w, x, g_post.reshape(1, d), (g_next if want_h else g_post).reshape(1, d))
    return (outs[0], outs[1]) if want_h else (outs[0], None)


LOG2_E = math.log2(math.e)
SOFTPLUS2_LINEAR_ABOVE = 64.0
MASKED_EXPONENT = -1e30
SKIP_CARRY = 160.0


def _softplus2(z):
    return jnp.where(z > SOFTPLUS2_LINEAR_ABOVE, z, jnp.log(1.0 + jnp.exp2(z)) * LOG2_E)


def _sb_attn_kernel(q_ref, k_ref, v_ref, gate_ref, o_ref, acc_ref, carry_ref, sgate_ref, low_ref,
                    *, tq, bk, n_heads):
    i = pl.program_id(1)
    top = 2 * i + 1
    r = lax.broadcasted_iota(jnp.int32, (bk, bk), 0)
    c = lax.broadcasted_iota(jnp.int32, (bk, bk), 1)
    suffix = (r >= c).astype(BF16)
    before = c < r

    def key_rows(kb):
        return pl.ds(pl.multiple_of(kb * bk, bk), bk)

    def head_cols(head):
        return pl.ds(head * HEAD_DIM, HEAD_DIM)

    def scores(head, q_rows, kb, diagonal):
        z = lax.dot_general(q_ref[q_rows, head_cols(head)], k_ref[key_rows(kb), head_cols(head)],
                            (((1,), (1,)), ((), ())), preferred_element_type=F32)
        sp = _softplus2(z)
        if diagonal:
            sp = jnp.where(before, sp, 0.0)
        s_in = jnp.dot(sp.astype(BF16), suffix, preferred_element_type=F32)
        expo = jnp.minimum(z - s_in, 0.0)
        if diagonal:
            expo = jnp.where(before, expo, MASKED_EXPONENT)
        return expo, s_in[:, 0:1]

    def accumulate(head, q_rows, kb, expo, tot):
        carry = carry_ref[head, q_rows, :]
        a = jnp.exp2(expo - carry)
        acc_ref[q_rows, head_cols(head)] += jnp.dot(a.astype(BF16), v_ref[key_rows(kb), head_cols(head)],
                                                    preferred_element_type=F32)
        carry_ref[head, q_rows, :] = carry + tot

    def run_blocks(blocks):
        work = [(head,) + blk for blk in blocks for head in range(n_heads)]
        parts = [scores(*item) for item in work]
        for (head, q_rows, kb, _), (expo, tot) in zip(work, parts):
            accumulate(head, q_rows, kb, expo, tot)

    def min_carry():
        return jnp.min(carry_ref[...])

    acc_ref[...] = jnp.zeros_like(acc_ref)
    carry_ref[...] = jnp.zeros_like(carry_ref)
    early_rows = pl.ds(0, bk)
    late_rows = pl.ds(bk, bk)
    all_rows = pl.ds(0, tq)

    def first_blocks(blocks):
        run_blocks(blocks)
        low_ref[0] = min_carry()
        gate = gate_ref[...].astype(F32)
        sgate_ref[...] = gate * jax.nn.sigmoid(gate)

    @pl.when(i == 0)
    def _():
        first_blocks([(late_rows, top, True), (early_rows, top - 1, True), (late_rows, top - 1, False)])

    @pl.when(i > 0)
    def _():
        first_blocks([(late_rows, top, True), (early_rows, top - 1, True),
                      (late_rows, top - 1, False), (early_rows, top - 2, False)])

    @pl.when((i > 0) & (low_ref[0] < SKIP_CARRY))
    def _():
        run_blocks([(late_rows, top - 2, False)])
        low_ref[0] = min_carry()

    def keep_going(state):
        kb, lowest = state
        return (kb >= 0) & (lowest < SKIP_CARRY)

    def body(state):
        kb, _ = state
        run_blocks([(all_rows, kb, False)])
        return kb - 1, min_carry()

    lax.while_loop(keep_going, body, (top - 3, low_ref[0]))
    o_ref[...] = (acc_ref[...] * sgate_ref[...]).astype(o_ref.dtype)


def _sb_attention(qg, kv, *, bk=256, heads_per_step=4):
    s, two_w = qg.shape
    n_groups = (two_w // 2) // (heads_per_step * HEAD_DIM)
    tq = 2 * bk
    gw = heads_per_step * HEAD_DIM
    kern = functools.partial(_sb_attn_kernel, tq=tq, bk=bk, n_heads=heads_per_step)
    vmem = 2 * (2 * s * gw * 2 + 3 * tq * gw * 2) + 2 * tq * gw * 4 + heads_per_step * 16 * tq * bk * 4
    return pl.pallas_call(
        kern,
        grid=(n_groups, s // tq),
        in_specs=[
            pl.BlockSpec((tq, gw), lambda h, i: (i, h)),
            pl.BlockSpec((s, gw), lambda h, i: (0, h)),
            pl.BlockSpec((s, gw), lambda h, i: (0, n_groups + h)),
            pl.BlockSpec((tq, gw), lambda h, i: (i, n_groups + h)),
        ],
        out_specs=pl.BlockSpec((tq, gw), lambda h, i: (i, h)),
        out_shape=jax.ShapeDtypeStruct((s, n_groups * gw), BF16),
        scratch_shapes=[pltpu.VMEM((tq, gw), F32), pltpu.VMEM((heads_per_step, tq, 1), F32),
                        pltpu.VMEM((tq, gw), F32), pltpu.SMEM((1,), F32)],
        compiler_params=pltpu.CompilerParams(
            dimension_semantics=("arbitrary", "arbitrary"),
            vmem_limit_bytes=_vmem_limit(vmem)),
        name="sb_attention",
    )(qg, kv, kv, qg)


def kernel(x, a_pre_norm, a_w_in, a_w_group, a_scale, a_w_out, a_post_norm, kv_norm, w_kv,
           b_pre_norm, b_w_in, b_w_out, b_post_norm):
    b, s, d = x.shape
    assert b == 1
    n_a, n_b = a_w_in.shape[0], b_w_in.shape[0]
    a_w_out, b_w_out = a_w_out.astype(BF16), b_w_out.astype(BF16)
    w_kv = w_kv[None]

    xs = x.reshape(s, d)
    h = _norm(xs, a_pre_norm[0])
    for layer in range(n_a):
        ug = _matmul(h, a_w_in, layer)
        mixed = _pool_mix(ug, a_w_group, a_scale[layer], layer)
        g_next = a_pre_norm[layer + 1] if layer + 1 < n_a else b_pre_norm[0]
        xs, h = _out_proj(mixed, a_w_out, layer, xs, a_post_norm[layer], g_next)
    kv = _matmul(_norm(xs, kv_norm), w_kv, 0)
    for layer in range(n_b):
        qg = _matmul(h, b_w_in, layer, q_scale=LOG2_E / math.sqrt(HEAD_DIM))
        og = _sb_attention(qg, kv)
        g_next = b_pre_norm[layer + 1] if layer + 1 < n_b else None
        xs, h = _out_proj(og, b_w_out, layer, xs, b_post_norm[layer], g_next)
    return xs.reshape(b, s, d)
```

```python
import functools
import math

import jax
import jax.numpy as jnp
from jax import lax
from jax.experimental import pallas as pl
from jax.experimental.pallas import tpu as pltpu

RMS_EPS = 1e-6
HEAD_DIM = 128

V7X_VMEM_BYTES = 64 * 1024 * 1024

BF16 = jnp.bfloat16
F32 = jnp.float32
NORM_ROW_CHUNK = 32
OUT_PROJ_SLAB = 512


def _vmem_limit(nbytes):
    return int(min(V7X_VMEM_BYTES - 6 * 1024 * 1024, nbytes + 8 * 1024 * 1024))


def _rms_scale(x):
    return lax.rsqrt(jnp.mean(x * x, axis=-1, keepdims=True) + RMS_EPS)


def _norm_kernel(x_ref, g_ref, o_ref):
    x = x_ref[...]
    o_ref[...] = (x * _rms_scale(x) * g_ref[...]).astype(o_ref.dtype)


def _norm(x, g, *, tm=256):
    s, d = x.shape
    return pl.pallas_call(
        _norm_kernel,
        grid=(s // tm,),
        in_specs=[pl.BlockSpec((tm, d), lambda i: (i, 0)), pl.BlockSpec((1, d), lambda i: (0, 0))],
        out_specs=pl.BlockSpec((tm, d), lambda i: (i, 0)),
        out_shape=jax.ShapeDtypeStruct((s, d), BF16),
        compiler_params=pltpu.CompilerParams(dimension_semantics=("arbitrary",)),
        name="norm",
    )(x, g.reshape(1, d))


def _matmul_kernel(h_ref, w_ref, o_ref, *, q_tiles, q_scale):
    acc = jnp.dot(h_ref[...], w_ref[...].astype(h_ref.dtype), preferred_element_type=F32)
    if q_tiles:
        acc = acc * jnp.where(pl.program_id(1) < q_tiles, q_scale, 1.0).astype(F32)
    o_ref[...] = acc.astype(o_ref.dtype)


def _matmul(h, w, layer, *, q_scale=None, tm=2048, tn=256):
    s, d = h.shape
    n = w.shape[2]
    q_tiles = (n // 2) // tn if q_scale is not None else 0
    kern = functools.partial(_matmul_kernel, q_tiles=q_tiles, q_scale=q_scale if q_scale is not None else 1.0)
    vmem = 2 * tm * d * 2 + 2 * d * tn * w.dtype.itemsize + d * tn * 2 + 2 * tm * tn * 2 + tm * tn * 4
    return pl.pallas_call(
        kern,
        grid=(s // tm, n // tn),
        in_specs=[
            pl.BlockSpec((tm, d), lambda i, j: (i, 0)),
            pl.BlockSpec((None, d, tn), lambda i, j: (layer, 0, j)),
        ],
        out_specs=pl.BlockSpec((tm, tn), lambda i, j: (i, j)),
        out_shape=jax.ShapeDtypeStruct((s, n), BF16),
        compiler_params=pltpu.CompilerParams(
            dimension_semantics=("arbitrary", "arbitrary"),
            vmem_limit_bytes=_vmem_limit(vmem)),
        name="in_proj",
    )(h, w)


POOL_SUB = 128


def _pool_kernel(u_ref, halo_ref, gate_ref, wg_ref, scale_ref, o_ref, pooled_ref):
    g = pl.program_id(0)
    i = pl.program_id(1)
    tm = u_ref.shape[0]
    w = lax.shift_left(jnp.int32(2), g)

    r = lax.broadcasted_iota(jnp.int32, (POOL_SUB, POOL_SUB), 0)
    c = lax.broadcasted_iota(jnp.int32, (POOL_SUB, POOL_SUB), 1)
    d_cur = r - c
    d_prev = d_cur + POOL_SUB
    band_cur = ((d_cur >= 0) & (d_cur < w)).astype(BF16)
    band_prev = (d_prev < w).astype(BF16)
    band = jnp.concatenate([band_prev, band_cur], axis=1)

    for rb in range(tm // POOL_SUB):
        rows = slice(rb * POOL_SUB, (rb + 1) * POOL_SUB)
        cur = u_ref[rows, :]
        if rb == 0:
            halo = halo_ref[...]
            prev_and_cur = jnp.concatenate([jnp.where(i > 0, halo, jnp.zeros_like(halo)), cur], axis=0)
        else:
            prev_and_cur = u_ref[(rb - 1) * POOL_SUB:(rb + 1) * POOL_SUB, :]
        wsum = jnp.dot(band, prev_and_cur, preferred_element_type=F32)
        t = i * tm + rb * POOL_SUB + lax.broadcasted_iota(jnp.int32, (POOL_SUB, 1), 0)
        inv_count = 1.0 / jnp.minimum(t + 1, w).astype(F32)
        pooled_ref[rows, :] = (wsum * inv_count - cur.astype(F32)).astype(pooled_ref.dtype)

    mixed = jnp.dot(pooled_ref[...], wg_ref[...].astype(BF16), preferred_element_type=F32)
    gate = gate_ref[...].astype(F32)
    o_ref[...] = (mixed * scale_ref[...] * (gate * jax.nn.sigmoid(gate))).astype(o_ref.dtype)


def _pool_mix(ug, w_group, scale, layer, *, tm=512):
    s = ug.shape[0]
    _, n_groups, gd, _ = w_group.shape
    width = n_groups * gd
    sub_per_tile = tm // POOL_SUB
    vmem = (2 * (2 * tm * gd * 2 + POOL_SUB * gd * 2 + gd * gd * w_group.dtype.itemsize + tm * gd * 2)
            + tm * gd * 2 + gd * gd * 2 + 2 * tm * gd * 4)
    return pl.pallas_call(
        _pool_kernel,
        grid=(n_groups, s // tm),
        in_specs=[
            pl.BlockSpec((tm, gd), lambda g, i: (i, g)),
            pl.BlockSpec((POOL_SUB, gd), lambda g, i: (jnp.maximum(i * sub_per_tile - 1, 0), g)),
            pl.BlockSpec((tm, gd), lambda g, i: (i, n_groups + g)),
            pl.BlockSpec((None, None, gd, gd), lambda g, i: (layer, g, 0, 0)),
            pl.BlockSpec((1, gd), lambda g, i: (0, g)),
        ],
        out_specs=pl.BlockSpec((tm, gd), lambda g, i: (i, g)),
        out_shape=jax.ShapeDtypeStruct((s, width), BF16),
        scratch_shapes=[pltpu.VMEM((tm, gd), BF16)],
        compiler_params=pltpu.CompilerParams(
            dimension_semantics=("arbitrary", "arbitrary"),
            vmem_limit_bytes=_vmem_limit(vmem)),
        name="pool_mix",
    )(ug, ug, ug, w_group, scale.reshape(1, width))


def _out_proj_kernel(a_ref, w_ref, x_ref, gpost_ref, gnext_ref, xo_ref, *rest, tn):
    *maybe_ho_ref, xfull_ref = rest
    j = pl.program_id(1)
    tm = xo_ref.shape[0]
    cols = pl.ds(pl.multiple_of(j * tn, tn), tn)
    xo_ref[:, cols] = jnp.dot(a_ref[...], w_ref[...], preferred_element_type=F32)
    xfull_ref[:, cols] = x_ref[...]

    @pl.when(j == pl.num_programs(1) - 1)
    def _():
        gpost = gpost_ref[...]
        gnext = gnext_ref[...]

        for c in range(tm // NORM_ROW_CHUNK):
            r = pl.ds(c * NORM_ROW_CHUNK, NORM_ROW_CHUNK)
            y = xo_ref[r, :]
            xn = xfull_ref[r, :] + y * _rms_scale(y) * gpost
            xo_ref[r, :] = xn
            for ho_ref in maybe_ho_ref:
                ho_ref[r, :] = (xn * _rms_scale(xn) * gnext).astype(ho_ref.dtype)


def _slab_major(w, tn):
    layers, kdim, n = w.shape
    return w.reshape(layers, kdim, n // tn, tn).transpose(0, 2, 1, 3)


def _out_proj(a, w, layer, x, g_post, g_next=None, *, tm=512):
    s, kdim = a.shape
    _, n_slabs, _, tn = w.shape
    d = n_slabs * tn
    want_h = g_next is not None
    kern = functools.partial(_out_proj_kernel, tn=tn)
    out_shape = [jax.ShapeDtypeStruct((s, d), F32)]
    out_specs = [pl.BlockSpec((tm, d), lambda i, j: (i, 0))]
    if want_h:
        out_shape.append(jax.ShapeDtypeStruct((s, d), BF16))
        out_specs.append(pl.BlockSpec((tm, d), lambda i, j: (i, 0)))
    vmem = (2 * tm * kdim * 2 + 2 * kdim * tn * 2 + 2 * tm * tn * 4 + tm * d * 4 + 2 * tm * d * 4
            + (2 * tm * d * 2 if want_h else 0) + tm * tn * 4)
    outs = pl.pallas_call(
        kern,
        grid=(s // tm, d // tn),
        in_specs=[
            pl.BlockSpec((tm, kdim), lambda i, j: (i, 0)),
            pl.BlockSpec((None, None, kdim, tn), lambda i, j: (layer, j, 0, 0)),
            pl.BlockSpec((tm, tn), lambda i, j: (i, j)),
            pl.BlockSpec((1, d), lambda i, j: (0, 0)),
            pl.BlockSpec((1, d), lambda i, j: (0, 0)),
        ],
        out_specs=out_specs,
        out_shape=out_shape,
        scratch_shapes=[pltpu.VMEM((tm, d), F32)],
        compiler_params=pltpu.CompilerParams(
            dimension_semantics=("arbitrary", "arbitrary"),
            vmem_limit_bytes=_vmem_limit(vmem)),
        name="out_proj",
    )(a, w, x, g_post.reshape(1, d), (g_next if want_h else g_post).reshape(1, d))
    return (outs[0], outs[1]) if want_h else (outs[0], None)


LOG2_E = math.log2(math.e)
SOFTPLUS2_LINEAR_ABOVE = 64.0
MASKED_EXPONENT = -1e30
SKIP_CARRY = 160.0


def _softplus2(z):
    return jnp.where(z > SOFTPLUS2_LINEAR_ABOVE, z, jnp.log(1.0 + jnp.exp2(z)) * LOG2_E)


def _sb_attn_kernel(q_ref, k_ref, v_ref, gate_ref, o_ref, acc_ref, carry_ref, sgate_ref, low_ref,
                    *, tq, bk, n_heads):
    i = pl.program_id(1)
    top = 2 * i + 1
    r = lax.broadcasted_iota(jnp.int32, (bk, bk), 0)
    c = lax.broadcasted_iota(jnp.int32, (bk, bk), 1)
    suffix = (r >= c).astype(BF16)
    before = c < r

    def key_rows(kb):
        return pl.ds(pl.multiple_of(kb * bk, bk), bk)

    def head_cols(head):
        return pl.ds(head * HEAD_DIM, HEAD_DIM)

    def scores(head, q_rows, kb, diagonal):
        z = lax.dot_general(q_ref[q_rows, head_cols(head)], k_ref[key_rows(kb), head_cols(head)],
                            (((1,), (1,)), ((), ())), preferred_element_type=F32)
        sp = _softplus2(z)
        if diagonal:
            sp = jnp.where(before, sp, 0.0)
        s_in = jnp.dot(sp.astype(BF16), suffix, preferred_element_type=F32)
        expo = jnp.minimum(z - s_in, 0.0)
        if diagonal:
            expo = jnp.where(before, expo, MASKED_EXPONENT)
        return expo, s_in[:, 0:1]

    def accumulate(head, q_rows, kb, expo, tot):
        carry = carry_ref[head, q_rows, :]
        a = jnp.exp2(expo - carry)
        acc_ref[q_rows, head_cols(head)] += jnp.dot(a.astype(BF16), v_ref[key_rows(kb), head_cols(head)],
                                                    preferred_element_type=F32)
        carry_ref[head, q_rows, :] = carry + tot

    def run_blocks(blocks):
        work = [(head,) + blk for blk in blocks for head in range(n_heads)]
        parts = [scores(*item) for item in work]
        for (head, q_rows, kb, _), (expo, tot) in zip(work, parts):
            accumulate(head, q_rows, kb, expo, tot)

    def min_carry():
        return jnp.min(carry_ref[...])

    acc_ref[...] = jnp.zeros_like(acc_ref)
    carry_ref[...] = jnp.zeros_like(carry_ref)
    early_rows = pl.ds(0, bk)
    late_rows = pl.ds(bk, bk)
    all_rows = pl.ds(0, tq)

    def first_blocks(blocks):
        run_blocks(blocks)
        low_ref[0] = min_carry()
        gate = gate_ref[...].astype(F32)
        sgate_ref[...] = gate * jax.nn.sigmoid(gate)

    @pl.when(i == 0)
    def _():
        first_blocks([(late_rows, top, True), (early_rows, top - 1, True), (late_rows, top - 1, False)])

    @pl.when(i > 0)
    def _():
        first_blocks([(late_rows, top, True), (early_rows, top - 1, True),
                      (late_rows, top - 1, False), (early_rows, top - 2, False)])

    @pl.when((i > 0) & (low_ref[0] < SKIP_CARRY))
    def _():
        run_blocks([(late_rows, top - 2, False)])
        low_ref[0] = min_carry()

    def keep_going(state):
        kb, lowest = state
        return (kb >= 0) & (lowest < SKIP_CARRY)

    def body(state):
        kb, _ = state
        run_blocks([(all_rows, kb, False)])
        return kb - 1, min_carry()

    lax.while_loop(keep_going, body, (top - 3, low_ref[0]))
    o_ref[...] = (acc_ref[...] * sgate_ref[...]).astype(o_ref.dtype)


def _sb_attention(qg, kv, *, bk=256, heads_per_step=4):
    s, two_w = qg.shape
    n_groups = (two_w // 2) // (heads_per_step * HEAD_DIM)
    tq = 2 * bk
    gw = heads_per_step * HEAD_DIM
    kern = functools.partial(_sb_attn_kernel, tq=tq, bk=bk, n_heads=heads_per_step)
    vmem = 2 * (2 * s * gw * 2 + 3 * tq * gw * 2) + 2 * tq * gw * 4 + heads_per_step * 16 * tq * bk * 4
    return pl.pallas_call(
        kern,
        grid=(n_groups, s // tq),
        in_specs=[
            pl.BlockSpec((tq, gw), lambda h, i: (i, h)),
            pl.BlockSpec((s, gw), lambda h, i: (0, h)),
            pl.BlockSpec((s, gw), lambda h, i: (0, n_groups + h)),
            pl.BlockSpec((tq, gw), lambda h, i: (i, n_groups + h)),
        ],
        out_specs=pl.BlockSpec((tq, gw), lambda h, i: (i, h)),
        out_shape=jax.ShapeDtypeStruct((s, n_groups * gw), BF16),
        scratch_shapes=[pltpu.VMEM((tq, gw), F32), pltpu.VMEM((heads_per_step, tq, 1), F32),
                        pltpu.VMEM((tq, gw), F32), pltpu.SMEM((1,), F32)],
        compiler_params=pltpu.CompilerParams(
            dimension_semantics=("arbitrary", "arbitrary"),
            vmem_limit_bytes=_vmem_limit(vmem)),
        name="sb_attention",
    )(qg, kv, kv, qg)


def kernel(x, a_pre_norm, a_w_in, a_w_group, a_scale, a_w_out, a_post_norm, kv_norm, w_kv,
           b_pre_norm, b_w_in, b_w_out, b_post_norm):
    b, s, d = x.shape
    assert b == 1
    n_a, n_b = a_w_in.shape[0], b_w_in.shape[0]
    a_w_out = _slab_major(a_w_out.astype(BF16), OUT_PROJ_SLAB)
    b_w_out = _slab_major(b_w_out.astype(BF16), OUT_PROJ_SLAB)
    w_kv = w_kv[None]

    xs = x.reshape(s, d)
    h = _norm(xs, a_pre_norm[0])
    for layer in range(n_a):
        ug = _matmul(h, a_w_in, layer)
        mixed = _pool_mix(ug, a_w_group, a_scale[layer], layer)
        g_next = a_pre_norm[layer + 1] if layer + 1 < n_a else b_pre_norm[0]
        xs, h = _out_proj(mixed, a_w_out, layer, xs, a_post_norm[layer], g_next)
    kv = _matmul(_norm(xs, kv_norm), w_kv, 0)
    for layer in range(n_b):
        qg = _matmul(h, b_w_in, layer, q_scale=LOG2_E / math.sqrt(HEAD_DIM))
        og = _sb_attention(qg, kv)
        g_next = b_pre_norm[layer + 1] if layer + 1 < n_b else None
        xs, h = _out_proj(og, b_w_out, layer, xs, b_post_norm[layer], g_next)
    return xs.reshape(b, s, d)
```

```python
import functools
import math

import jax
import jax.numpy as jnp
from jax import lax
from jax.experimental import pallas as pl
from jax.experimental.pallas import tpu as pltpu

RMS_EPS = 1e-6
HEAD_DIM = 128

V7X_VMEM_BYTES = 64 * 1024 * 1024

BF16 = jnp.bfloat16
F32 = jnp.float32
NORM_ROW_CHUNK = 32


def _vmem_limit(nbytes):
    return int(min(V7X_VMEM_BYTES - 6 * 1024 * 1024, nbytes + 8 * 1024 * 1024))


def _rms_scale(x):
    return lax.rsqrt(jnp.mean(x * x, axis=-1, keepdims=True) + RMS_EPS)


def _norm_kernel(x_ref, g_ref, o_ref):
    x = x_ref[...]
    o_ref[...] = (x * _rms_scale(x) * g_ref[...]).astype(o_ref.dtype)


def _norm(x, g, *, tm=256):
    s, d = x.shape
    return pl.pallas_call(
        _norm_kernel,
        grid=(s // tm,),
        in_specs=[pl.BlockSpec((tm, d), lambda i: (i, 0)), pl.BlockSpec((1, d), lambda i: (0, 0))],
        out_specs=pl.BlockSpec((tm, d), lambda i: (i, 0)),
        out_shape=jax.ShapeDtypeStruct((s, d), BF16),
        compiler_params=pltpu.CompilerParams(dimension_semantics=("arbitrary",)),
        name="norm",
    )(x, g.reshape(1, d))


def _matmul_kernel(h_ref, w_ref, *rest, q_tiles, q_scale):
    if len(rest) == 3:
        cast_src_ref, o_ref, cast_dst_ref = rest
        cast_dst_ref[...] = cast_src_ref[...].astype(cast_dst_ref.dtype)
    else:
        (o_ref,) = rest
    acc = jnp.dot(h_ref[...], w_ref[...].astype(h_ref.dtype), preferred_element_type=F32)
    if q_tiles:
        acc = acc * jnp.where(pl.program_id(1) < q_tiles, q_scale, 1.0).astype(F32)
    o_ref[...] = acc.astype(o_ref.dtype)


def _matmul(h, w, layer, *, q_scale=None, also_cast=None, tm=1024, tn=512):
    s, d = h.shape
    n = w.shape[2]
    n_i, n_j = s // tm, n // tn
    q_tiles = (n // 2) // tn if q_scale is not None else 0
    kern = functools.partial(_matmul_kernel, q_tiles=q_tiles, q_scale=q_scale if q_scale is not None else 1.0)
    vmem = 2 * tm * d * 2 + 2 * d * tn * w.dtype.itemsize + d * tn * 2 + 2 * tm * tn * 2 + tm * tn * 4
    operands = [h, w]
    in_specs = [
        pl.BlockSpec((tm, d), lambda i, j: (i, 0)),
        pl.BlockSpec((None, d, tn), lambda i, j: (layer, 0, j)),
    ]
    out_specs = [pl.BlockSpec((tm, tn), lambda i, j: (i, j))]
    out_shape = [jax.ShapeDtypeStruct((s, n), BF16)]
    if also_cast is not None:
        cols = also_cast.shape[-1]
        flat = also_cast.reshape(-1, cols)
        rows_per_step = flat.shape[0] // (n_i * n_j)
        assert rows_per_step * n_i * n_j == flat.shape[0]
        operands.append(flat)
        cast_spec = pl.BlockSpec((rows_per_step, cols), lambda i, j: (i * n_j + j, 0))
        in_specs.append(cast_spec)
        out_specs.append(cast_spec)
        out_shape.append(jax.ShapeDtypeStruct(flat.shape, BF16))
        vmem += 2 * rows_per_step * cols * (4 + 2)
    outs = pl.pallas_call(
        kern,
        grid=(n_i, n_j),
        in_specs=in_specs,
        out_specs=out_specs,
        out_shape=out_shape,
        compiler_params=pltpu.CompilerParams(
            dimension_semantics=("arbitrary", "arbitrary"),
            vmem_limit_bytes=_vmem_limit(vmem)),
        name="in_proj",
    )(*operands)
    if also_cast is None:
        return outs[0]
    return outs[0], outs[1].reshape(also_cast.shape)


POOL_SUB = 128


def _pool_kernel(u_ref, halo_ref, gate_ref, wg_ref, scale_ref, o_ref, pooled_ref):
    g = pl.program_id(0)
    i = pl.program_id(1)
    tm = u_ref.shape[0]
    w = lax.shift_left(jnp.int32(2), g)

    r = lax.broadcasted_iota(jnp.int32, (POOL_SUB, POOL_SUB), 0)
    c = lax.broadcasted_iota(jnp.int32, (POOL_SUB, POOL_SUB), 1)
    d_cur = r - c
    d_prev = d_cur + POOL_SUB
    band_cur = ((d_cur >= 0) & (d_cur < w)).astype(BF16)
    band_prev = (d_prev < w).astype(BF16)
    band = jnp.concatenate([band_prev, band_cur], axis=1)

    for rb in range(tm // POOL_SUB):
        rows = slice(rb * POOL_SUB, (rb + 1) * POOL_SUB)
        cur = u_ref[rows, :]
        if rb == 0:
            halo = halo_ref[...]
            prev_and_cur = jnp.concatenate([jnp.where(i > 0, halo, jnp.zeros_like(halo)), cur], axis=0)
        else:
            prev_and_cur = u_ref[(rb - 1) * POOL_SUB:(rb + 1) * POOL_SUB, :]
        wsum = jnp.dot(band, prev_and_cur, preferred_element_type=F32)
        t = i * tm + rb * POOL_SUB + lax.broadcasted_iota(jnp.int32, (POOL_SUB, 1), 0)
        inv_count = 1.0 / jnp.minimum(t + 1, w).astype(F32)
        pooled_ref[rows, :] = (wsum * inv_count - cur.astype(F32)).astype(pooled_ref.dtype)

    mixed = jnp.dot(pooled_ref[...], wg_ref[...].astype(BF16), preferred_element_type=F32)
    gate = gate_ref[...].astype(F32)
    o_ref[...] = (mixed * scale_ref[...] * (gate * jax.nn.sigmoid(gate))).astype(o_ref.dtype)


def _pool_mix(ug, w_group, scale, layer, *, tm=512):
    s = ug.shape[0]
    _, n_groups, gd, _ = w_group.shape
    width = n_groups * gd
    sub_per_tile = tm // POOL_SUB
    vmem = (2 * (2 * tm * gd * 2 + POOL_SUB * gd * 2 + gd * gd * w_group.dtype.itemsize + tm * gd * 2)
            + tm * gd * 2 + gd * gd * 2 + 2 * tm * gd * 4)
    return pl.pallas_call(
        _pool_kernel,
        grid=(n_groups, s // tm),
        in_specs=[
            pl.BlockSpec((tm, gd), lambda g, i: (i, g)),
            pl.BlockSpec((POOL_SUB, gd), lambda g, i: (jnp.maximum(i * sub_per_tile - 1, 0), g)),
            pl.BlockSpec((tm, gd), lambda g, i: (i, n_groups + g)),
            pl.BlockSpec((None, None, gd, gd), lambda g, i: (layer, g, 0, 0)),
            pl.BlockSpec((1, gd), lambda g, i: (0, g)),
        ],
        out_specs=pl.BlockSpec((tm, gd), lambda g, i: (i, g)),
        out_shape=jax.ShapeDtypeStruct((s, width), BF16),
        scratch_shapes=[pltpu.VMEM((tm, gd), BF16)],
        compiler_params=pltpu.CompilerParams(
            dimension_semantics=("arbitrary", "arbitrary"),
            vmem_limit_bytes=_vmem_limit(vmem)),
        name="pool_mix",
    )(ug, ug, ug, w_group, scale.reshape(1, width))


def _out_proj_kernel(a_ref, w_ref, x_ref, gpost_ref, gnext_ref, xo_ref, *rest, tn):
    *maybe_ho_ref, xfull_ref = rest
    j = pl.program_id(1)
    tm = xo_ref.shape[0]
    cols = pl.ds(pl.multiple_of(j * tn, tn), tn)
    xo_ref[:, cols] = jnp.dot(a_ref[...], w_ref[...], preferred_element_type=F32)
    xfull_ref[:, cols] = x_ref[...]

    @pl.when(j == pl.num_programs(1) - 1)
    def _():
        gpost = gpost_ref[...]
        gnext = gnext_ref[...]

        for c in range(tm // NORM_ROW_CHUNK):
            r = pl.ds(c * NORM_ROW_CHUNK, NORM_ROW_CHUNK)
            y = xo_ref[r, :]
            xn = xfull_ref[r, :] + y * _rms_scale(y) * gpost
            xo_ref[r, :] = xn
            for ho_ref in maybe_ho_ref:
                ho_ref[r, :] = (xn * _rms_scale(xn) * gnext).astype(ho_ref.dtype)


def _out_proj(a, w, layer, x, g_post, g_next=None, *, tm=512, tn=512):
    s, kdim = a.shape
    d = w.shape[2]
    want_h = g_next is not None
    kern = functools.partial(_out_proj_kernel, tn=tn)
    out_shape = [jax.ShapeDtypeStruct((s, d), F32)]
    out_specs = [pl.BlockSpec((tm, d), lambda i, j: (i, 0))]
    if want_h:
        out_shape.append(jax.ShapeDtypeStruct((s, d), BF16))
        out_specs.append(pl.BlockSpec((tm, d), lambda i, j: (i, 0)))
    vmem = (2 * tm * kdim * 2 + 2 * kdim * tn * 2 + 2 * tm * tn * 4 + tm * d * 4 + 2 * tm * d * 4
            + (2 * tm * d * 2 if want_h else 0) + tm * tn * 4)
    outs = pl.pallas_call(
        kern,
        grid=(s // tm, d // tn),
        in_specs=[
            pl.BlockSpec((tm, kdim), lambda i, j: (i, 0)),
            pl.BlockSpec((None, kdim, tn), lambda i, j: (layer, 0, j)),
            pl.BlockSpec((tm, tn), lambda i, j: (i, j)),
            pl.BlockSpec((1, d), lambda i, j: (0, 0)),
            pl.BlockSpec((1, d), lambda i, j: (0, 0)),
        ],
        out_specs=out_specs,
        out_shape=out_shape,
        scratch_shapes=[pltpu.VMEM((tm, d), F32)],
        compiler_params=pltpu.CompilerParams(
            dimension_semantics=("arbitrary", "arbitrary"),
            vmem_limit_bytes=_vmem_limit(vmem)),
        name="out_proj",
    )(a, w, x, g_post.reshape(1, d), (g_next if want_h else g_post).reshape(1, d))
    return (outs[0], outs[1]) if want_h else (outs[0], None)


LOG2_E = math.log2(math.e)
SOFTPLUS2_LINEAR_ABOVE = 64.0
MASKED_EXPONENT = -1e30
SKIP_CARRY = 160.0


def _softplus2(z):
    return jnp.where(z > SOFTPLUS2_LINEAR_ABOVE, z, jnp.log(1.0 + jnp.exp2(z)) * LOG2_E)


def _sb_attn_kernel(q_ref, k_ref, v_ref, gate_ref, o_ref, acc_ref, carry_ref, sgate_ref, low_ref,
                    *, tq, bk, n_heads):
    i = pl.program_id(1)
    top = 2 * i + 1
    r = lax.broadcasted_iota(jnp.int32, (bk, bk), 0)
    c = lax.broadcasted_iota(jnp.int32, (bk, bk), 1)
    suffix = (r >= c).astype(BF16)
    before = c < r

    def key_rows(kb):
        return pl.ds(pl.multiple_of(kb * bk, bk), bk)

    def head_cols(head):
        return pl.ds(head * HEAD_DIM, HEAD_DIM)

    def scores(head, q_rows, kb, diagonal):
        z = lax.dot_general(q_ref[q_rows, head_cols(head)], k_ref[key_rows(kb), head_cols(head)],
                            (((1,), (1,)), ((), ())), preferred_element_type=F32)
        sp = _softplus2(z)
        if diagonal:
            sp = jnp.where(before, sp, 0.0)
        s_in = jnp.dot(sp.astype(BF16), suffix, preferred_element_type=F32)
        expo = jnp.minimum(z - s_in, 0.0)
        if diagonal:
            expo = jnp.where(before, expo, MASKED_EXPONENT)
        return expo, s_in[:, 0:1]

    def accumulate(head, q_rows, kb, expo, tot):
        carry = carry_ref[head, q_rows, :]
        a = jnp.exp2(expo - carry)
        acc_ref[q_rows, head_cols(head)] += jnp.dot(a.astype(BF16), v_ref[key_rows(kb), head_cols(head)],
                                                    preferred_element_type=F32)
        carry_ref[head, q_rows, :] = carry + tot

    def run_blocks(blocks):
        work = [(head,) + blk for blk in blocks for head in range(n_heads)]
        parts = [scores(*item) for item in work]
        for (head, q_rows, kb, _), (expo, tot) in zip(work, parts):
            accumulate(head, q_rows, kb, expo, tot)

    def min_carry():
        return jnp.min(carry_ref[...])

    acc_ref[...] = jnp.zeros_like(acc_ref)
    carry_ref[...] = jnp.zeros_like(carry_ref)
    early_rows = pl.ds(0, bk)
    late_rows = pl.ds(bk, bk)
    all_rows = pl.ds(0, tq)

    def first_blocks(blocks):
        run_blocks(blocks)
        low_ref[0] = min_carry()
        gate = gate_ref[...].astype(F32)
        sgate_ref[...] = gate * jax.nn.sigmoid(gate)

    @pl.when(i == 0)
    def _():
        first_blocks([(late_rows, top, True), (early_rows, top - 1, True), (late_rows, top - 1, False)])

    @pl.when(i > 0)
    def _():
        first_blocks([(late_rows, top, True), (early_rows, top - 1, True),
                      (late_rows, top - 1, False), (early_rows, top - 2, False)])

    @pl.when((i > 0) & (low_ref[0] < SKIP_CARRY))
    def _():
        run_blocks([(late_rows, top - 2, False)])
        low_ref[0] = min_carry()

    def keep_going(state):
        kb, lowest = state
        return (kb >= 0) & (lowest < SKIP_CARRY)

    def body(state):
        kb, _ = state
        run_blocks([(all_rows, kb, False)])
        return kb - 1, min_carry()

    lax.while_loop(keep_going, body, (top - 3, low_ref[0]))
    o_ref[...] = (acc_ref[...] * sgate_ref[...]).astype(o_ref.dtype)


def _sb_attention(qg, kv, *, bk=256, heads_per_step=4):
    s, two_w = qg.shape
    n_groups = (two_w // 2) // (heads_per_step * HEAD_DIM)
    tq = 2 * bk
    gw = heads_per_step * HEAD_DIM
    kern = functools.partial(_sb_attn_kernel, tq=tq, bk=bk, n_heads=heads_per_step)
    vmem = 2 * (2 * s * gw * 2 + 3 * tq * gw * 2) + 2 * tq * gw * 4 + heads_per_step * 16 * tq * bk * 4
    return pl.pallas_call(
        kern,
        grid=(n_groups, s // tq),
        in_specs=[
            pl.BlockSpec((tq, gw), lambda h, i: (i, h)),
            pl.BlockSpec((s, gw), lambda h, i: (0, h)),
            pl.BlockSpec((s, gw), lambda h, i: (0, n_groups + h)),
            pl.BlockSpec((tq, gw), lambda h, i: (i, n_groups + h)),
        ],
        out_specs=pl.BlockSpec((tq, gw), lambda h, i: (i, h)),
        out_shape=jax.ShapeDtypeStruct((s, n_groups * gw), BF16),
        scratch_shapes=[pltpu.VMEM((tq, gw), F32), pltpu.VMEM((heads_per_step, tq, 1), F32),
                        pltpu.VMEM((tq, gw), F32), pltpu.SMEM((1,), F32)],
        compiler_params=pltpu.CompilerParams(
            dimension_semantics=("arbitrary", "arbitrary"),
            vmem_limit_bytes=_vmem_limit(vmem)),
        name="sb_attention",
    )(qg, kv, kv, qg)


def kernel(x, a_pre_norm, a_w_in, a_w_group, a_scale, a_w_out, a_post_norm, kv_norm, w_kv,
           b_pre_norm, b_w_in, b_w_out, b_post_norm):
    b, s, d = x.shape
    assert b == 1
    n_a, n_b = a_w_in.shape[0], b_w_in.shape[0]
    w_kv = w_kv[None]

    xs = x.reshape(s, d)
    h = _norm(xs, a_pre_norm[0])
    for layer in range(n_a):
        if layer == 0:
            ug, a_w_out = _matmul(h, a_w_in, layer, also_cast=a_w_out)
        else:
            ug = _matmul(h, a_w_in, layer)
        mixed = _pool_mix(ug, a_w_group, a_scale[layer], layer)
        g_next = a_pre_norm[layer + 1] if layer + 1 < n_a else b_pre_norm[0]
        xs, h = _out_proj(mixed, a_w_out, layer, xs, a_post_norm[layer], g_next)
    kv, b_w_out = _matmul(_norm(xs, kv_norm), w_kv, 0, also_cast=b_w_out)
    for layer in range(n_b):
        qg = _matmul(h, b_w_in, layer, q_scale=LOG2_E / math.sqrt(HEAD_DIM))
        og = _sb_attention(qg, kv)
        g_next = b_pre_norm[layer + 1] if layer + 1 < n_b else None
        xs, h = _out_proj(og, b_w_out, layer, xs, b_post_norm[layer], g_next)
    return xs.reshape(b, s, d)
```

```python
import functools
import math

import jax
import jax.numpy as jnp
from jax import lax
from jax.experimental import pallas as pl
from jax.experimental.pallas import tpu as pltpu

RMS_EPS = 1e-6
HEAD_DIM = 128

V7X_VMEM_BYTES = 64 * 1024 * 1024

BF16 = jnp.bfloat16
F32 = jnp.float32
NORM_ROW_CHUNK = 32


def _vmem_limit(nbytes):
    return int(min(V7X_VMEM_BYTES - 6 * 1024 * 1024, nbytes + 8 * 1024 * 1024))


def _rms_scale(x):
    return lax.rsqrt(jnp.mean(x * x, axis=-1, keepdims=True) + RMS_EPS)


RING_DEPTH = 3


def _weight_slab(w_hbm, ring_ref, sem, layer, tn):
    n_slabs = pl.num_programs(1)
    step = pl.program_id(0) * n_slabs + pl.program_id(1)
    n_steps = pl.num_programs(0) * n_slabs

    def fetch(s):
        start = (s % n_slabs) * tn
        cols = pl.ds(start if isinstance(start, int) else pl.multiple_of(start, tn), tn)
        slot = s % RING_DEPTH
        return pltpu.make_async_copy(w_hbm.at[layer, :, cols], ring_ref.at[slot], sem.at[slot])

    @pl.when(step == 0)
    def _():
        for s in range(RING_DEPTH - 1):
            fetch(s).start()

    @pl.when(step + (RING_DEPTH - 1) < n_steps)
    def _():
        fetch(step + (RING_DEPTH - 1)).start()

    fetch(step).wait()
    return ring_ref.at[step % RING_DEPTH]


def _norm_kernel(x_ref, g_ref, o_ref):
    x = x_ref[...]
    o_ref[...] = (x * _rms_scale(x) * g_ref[...]).astype(o_ref.dtype)


def _norm(x, g, *, tm=256):
    s, d = x.shape
    return pl.pallas_call(
        _norm_kernel,
        grid=(s // tm,),
        in_specs=[pl.BlockSpec((tm, d), lambda i: (i, 0)), pl.BlockSpec((1, d), lambda i: (0, 0))],
        out_specs=pl.BlockSpec((tm, d), lambda i: (i, 0)),
        out_shape=jax.ShapeDtypeStruct((s, d), BF16),
        compiler_params=pltpu.CompilerParams(dimension_semantics=("arbitrary",)),
        name="norm",
    )(x, g.reshape(1, d))


def _matmul_kernel(h_ref, w_hbm, *rest, q_tiles, q_scale, layer):
    *io_refs, ring_ref, sem = rest
    if len(io_refs) == 3:
        cast_src_ref, o_ref, cast_dst_ref = io_refs
        cast_dst_ref[...] = cast_src_ref[...].astype(cast_dst_ref.dtype)
    else:
        (o_ref,) = io_refs
    w_ref = _weight_slab(w_hbm, ring_ref, sem, layer, o_ref.shape[1])
    acc = jnp.dot(h_ref[...], w_ref[...].astype(h_ref.dtype), preferred_element_type=F32)
    if q_tiles:
        acc = acc * jnp.where(pl.program_id(1) < q_tiles, q_scale, 1.0).astype(F32)
    o_ref[...] = acc.astype(o_ref.dtype)


def _matmul(h, w, layer, *, q_scale=None, also_cast=None, tm=1024, tn=512):
    s, d = h.shape
    n = w.shape[2]
    n_i, n_j = s // tm, n // tn
    q_tiles = (n // 2) // tn if q_scale is not None else 0
    kern = functools.partial(_matmul_kernel, q_tiles=q_tiles, q_scale=q_scale if q_scale is not None else 1.0,
                             layer=layer)
    vmem = (2 * tm * d * 2 + RING_DEPTH * d * tn * w.dtype.itemsize + d * tn * 2 + 2 * tm * tn * 2
            + tm * tn * 4)
    operands = [h, w]
    in_specs = [
        pl.BlockSpec((tm, d), lambda i, j: (i, 0)),
        pl.BlockSpec(memory_space=pl.ANY),
    ]
    out_specs = [pl.BlockSpec((tm, tn), lambda i, j: (i, j))]
    out_shape = [jax.ShapeDtypeStruct((s, n), BF16)]
    if also_cast is not None:
        cols = also_cast.shape[-1]
        flat = also_cast.reshape(-1, cols)
        rows_per_step = flat.shape[0] // (n_i * n_j)
        assert rows_per_step * n_i * n_j == flat.shape[0]
        operands.append(flat)
        cast_spec = pl.BlockSpec((rows_per_step, cols), lambda i, j: (i * n_j + j, 0))
        in_specs.append(cast_spec)
        out_specs.append(cast_spec)
        out_shape.append(jax.ShapeDtypeStruct(flat.shape, BF16))
        vmem += 2 * rows_per_step * cols * (4 + 2)
    outs = pl.pallas_call(
        kern,
        grid=(n_i, n_j),
        in_specs=in_specs,
        out_specs=out_specs,
        out_shape=out_shape,
        scratch_shapes=[pltpu.VMEM((RING_DEPTH, d, tn), w.dtype), pltpu.SemaphoreType.DMA((RING_DEPTH,))],
        compiler_params=pltpu.CompilerParams(
            dimension_semantics=("arbitrary", "arbitrary"),
            vmem_limit_bytes=_vmem_limit(vmem)),
        name="in_proj",
    )(*operands)
    if also_cast is None:
        return outs[0]
    return outs[0], outs[1].reshape(also_cast.shape)


POOL_SUB = 128


def _pool_kernel(u_ref, halo_ref, gate_ref, wg_ref, scale_ref, o_ref, pooled_ref):
    g = pl.program_id(0)
    i = pl.program_id(1)
    tm = u_ref.shape[0]
    w = lax.shift_left(jnp.int32(2), g)

    r = lax.broadcasted_iota(jnp.int32, (POOL_SUB, POOL_SUB), 0)
    c = lax.broadcasted_iota(jnp.int32, (POOL_SUB, POOL_SUB), 1)
    d_cur = r - c
    d_prev = d_cur + POOL_SUB
    band_cur = ((d_cur >= 0) & (d_cur < w)).astype(BF16)
    band_prev = (d_prev < w).astype(BF16)
    band = jnp.concatenate([band_prev, band_cur], axis=1)

    for rb in range(tm // POOL_SUB):
        rows = slice(rb * POOL_SUB, (rb + 1) * POOL_SUB)
        cur = u_ref[rows, :]
        if rb == 0:
            halo = halo_ref[...]
            prev_and_cur = jnp.concatenate([jnp.where(i > 0, halo, jnp.zeros_like(halo)), cur], axis=0)
        else:
            prev_and_cur = u_ref[(rb - 1) * POOL_SUB:(rb + 1) * POOL_SUB, :]
        wsum = jnp.dot(band, prev_and_cur, preferred_element_type=F32)
        t = i * tm + rb * POOL_SUB + lax.broadcasted_iota(jnp.int32, (POOL_SUB, 1), 0)
        inv_count = 1.0 / jnp.minimum(t + 1, w).astype(F32)
        pooled_ref[rows, :] = (wsum * inv_count - cur.astype(F32)).astype(pooled_ref.dtype)

    mixed = jnp.dot(pooled_ref[...], wg_ref[...].astype(BF16), preferred_element_type=F32)
    gate = gate_ref[...].astype(F32)
    o_ref[...] = (mixed * scale_ref[...] * (gate * jax.nn.sigmoid(gate))).astype(o_ref.dtype)


def _pool_mix(ug, w_group, scale, layer, *, tm=512):
    s = ug.shape[0]
    _, n_groups, gd, _ = w_group.shape
    width = n_groups * gd
    sub_per_tile = tm // POOL_SUB
    vmem = (2 * (2 * tm * gd * 2 + POOL_SUB * gd * 2 + gd * gd * w_group.dtype.itemsize + tm * gd * 2)
            + tm * gd * 2 + gd * gd * 2 + 2 * tm * gd * 4)
    return pl.pallas_call(
        _pool_kernel,
        grid=(n_groups, s // tm),
        in_specs=[
            pl.BlockSpec((tm, gd), lambda g, i: (i, g)),
            pl.BlockSpec((POOL_SUB, gd), lambda g, i: (jnp.maximum(i * sub_per_tile - 1, 0), g)),
            pl.BlockSpec((tm, gd), lambda g, i: (i, n_groups + g)),
            pl.BlockSpec((None, None, gd, gd), lambda g, i: (layer, g, 0, 0)),
            pl.BlockSpec((1, gd), lambda g, i: (0, g)),
        ],
        out_specs=pl.BlockSpec((tm, gd), lambda g, i: (i, g)),
        out_shape=jax.ShapeDtypeStruct((s, width), BF16),
        scratch_shapes=[pltpu.VMEM((tm, gd), BF16)],
        compiler_params=pltpu.CompilerParams(
            dimension_semantics=("arbitrary", "arbitrary"),
            vmem_limit_bytes=_vmem_limit(vmem)),
        name="pool_mix",
    )(ug, ug, ug, w_group, scale.reshape(1, width))


def _out_proj_kernel(a_ref, w_hbm, x_ref, gpost_ref, gnext_ref, xo_ref, *rest, tn, layer):
    *maybe_ho_ref, xfull_ref, ring_ref, sem = rest
    j = pl.program_id(1)
    tm = xo_ref.shape[0]
    cols = pl.ds(pl.multiple_of(j * tn, tn), tn)
    w_ref = _weight_slab(w_hbm, ring_ref, sem, layer, tn)
    xo_ref[:, cols] = jnp.dot(a_ref[...], w_ref[...], preferred_element_type=F32)
    xfull_ref[:, cols] = x_ref[...]

    @pl.when(j == pl.num_programs(1) - 1)
    def _():
        gpost = gpost_ref[...]
        gnext = gnext_ref[...]

        for c in range(tm // NORM_ROW_CHUNK):
            r = pl.ds(c * NORM_ROW_CHUNK, NORM_ROW_CHUNK)
            y = xo_ref[r, :]
            xn = xfull_ref[r, :] + y * _rms_scale(y) * gpost
            xo_ref[r, :] = xn
            for ho_ref in maybe_ho_ref:
                ho_ref[r, :] = (xn * _rms_scale(xn) * gnext).astype(ho_ref.dtype)


def _out_proj(a, w, layer, x, g_post, g_next=None, *, tm=512, tn=512):
    s, kdim = a.shape
    d = w.shape[2]
    want_h = g_next is not None
    kern = functools.partial(_out_proj_kernel, tn=tn, layer=layer)
    out_shape = [jax.ShapeDtypeStruct((s, d), F32)]
    out_specs = [pl.BlockSpec((tm, d), lambda i, j: (i, 0))]
    if want_h:
        out_shape.append(jax.ShapeDtypeStruct((s, d), BF16))
        out_specs.append(pl.BlockSpec((tm, d), lambda i, j: (i, 0)))
    vmem = (2 * tm * kdim * 2 + RING_DEPTH * kdim * tn * 2 + 2 * tm * tn * 4 + tm * d * 4 + 2 * tm * d * 4
            + (2 * tm * d * 2 if want_h else 0) + tm * tn * 4)
    outs = pl.pallas_call(
        kern,
        grid=(s // tm, d // tn),
        in_specs=[
            pl.BlockSpec((tm, kdim), lambda i, j: (i, 0)),
            pl.BlockSpec(memory_space=pl.ANY),
            pl.BlockSpec((tm, tn), lambda i, j: (i, j)),
            pl.BlockSpec((1, d), lambda i, j: (0, 0)),
            pl.BlockSpec((1, d), lambda i, j: (0, 0)),
        ],
        out_specs=out_specs,
        out_shape=out_shape,
        scratch_shapes=[pltpu.VMEM((tm, d), F32), pltpu.VMEM((RING_DEPTH, kdim, tn), w.dtype),
                        pltpu.SemaphoreType.DMA((RING_DEPTH,))],
        compiler_params=pltpu.CompilerParams(
            dimension_semantics=("arbitrary", "arbitrary"),
            vmem_limit_bytes=_vmem_limit(vmem)),
        name="out_proj",
    )(a, w, x, g_post.reshape(1, d), (g_next if want_h else g_post).reshape(1, d))
    return (outs[0], outs[1]) if want_h else (outs[0], None)


LOG2_E = math.log2(math.e)
SOFTPLUS2_LINEAR_ABOVE = 64.0
MASKED_EXPONENT = -1e30
SKIP_CARRY = 160.0


def _softplus2(z):
    return jnp.where(z > SOFTPLUS2_LINEAR_ABOVE, z, jnp.log(1.0 + jnp.exp2(z)) * LOG2_E)


def _sb_attn_kernel(q_ref, k_ref, v_ref, gate_ref, o_ref, acc_ref, carry_ref, sgate_ref, low_ref,
                    *, tq, bk, n_heads):
    i = pl.program_id(1)
    top = 2 * i + 1
    r = lax.broadcasted_iota(jnp.int32, (bk, bk), 0)
    c = lax.broadcasted_iota(jnp.int32, (bk, bk), 1)
    suffix = (r >= c).astype(BF16)
    before = c < r

    def key_rows(kb):
        return pl.ds(pl.multiple_of(kb * bk, bk), bk)

    def head_cols(head):
        return pl.ds(head * HEAD_DIM, HEAD_DIM)

    def scores(head, q_rows, kb, diagonal):
        z = lax.dot_general(q_ref[q_rows, head_cols(head)], k_ref[key_rows(kb), head_cols(head)],
                            (((1,), (1,)), ((), ())), preferred_element_type=F32)
        sp = _softplus2(z)
        if diagonal:
            sp = jnp.where(before, sp, 0.0)
        s_in = jnp.dot(sp.astype(BF16), suffix, preferred_element_type=F32)
        expo = jnp.minimum(z - s_in, 0.0)
        if diagonal:
            expo = jnp.where(before, expo, MASKED_EXPONENT)
        return expo, s_in[:, 0:1]

    def accumulate(head, q_rows, kb, expo, tot):
        carry = carry_ref[head, q_rows, :]
        a = jnp.exp2(expo - carry)
        acc_ref[q_rows, head_cols(head)] += jnp.dot(a.astype(BF16), v_ref[key_rows(kb), head_cols(head)],
                                                    preferred_element_type=F32)
        carry_ref[head, q_rows, :] = carry + tot

    def run_blocks(blocks):
        work = [(head,) + blk for blk in blocks for head in range(n_heads)]
        parts = [scores(*item) for item in work]
        for (head, q_rows, kb, _), (expo, tot) in zip(work, parts):
            accumulate(head, q_rows, kb, expo, tot)

    def min_carry():
        return jnp.min(carry_ref[...])

    acc_ref[...] = jnp.zeros_like(acc_ref)
    carry_ref[...] = jnp.zeros_like(carry_ref)
    early_rows = pl.ds(0, bk)
    late_rows = pl.ds(bk, bk)
    all_rows = pl.ds(0, tq)

    def first_blocks(blocks):
        run_blocks(blocks)
        low_ref[0] = min_carry()
        gate = gate_ref[...].astype(F32)
        sgate_ref[...] = gate * jax.nn.sigmoid(gate)

    @pl.when(i == 0)
    def _():
        first_blocks([(late_rows, top, True), (early_rows, top - 1, True), (late_rows, top - 1, False)])

    @pl.when(i > 0)
    def _():
        first_blocks([(late_rows, top, True), (early_rows, top - 1, True),
                      (late_rows, top - 1, False), (early_rows, top - 2, False)])

    @pl.when((i > 0) & (low_ref[0] < SKIP_CARRY))
    def _():
        run_blocks([(late_rows, top - 2, False)])
        low_ref[0] = min_carry()

    def keep_going(state):
        kb, lowest = state
        return (kb >= 0) & (lowest < SKIP_CARRY)

    def body(state):
        kb, _ = state
        run_blocks([(all_rows, kb, False)])
        return kb - 1, min_carry()

    lax.while_loop(keep_going, body, (top - 3, low_ref[0]))
    o_ref[...] = (acc_ref[...] * sgate_ref[...]).astype(o_ref.dtype)


def _sb_attention(qg, kv, *, bk=256, heads_per_step=4):
    s, two_w = qg.shape
    n_groups = (two_w // 2) // (heads_per_step * HEAD_DIM)
    tq = 2 * bk
    gw = heads_per_step * HEAD_DIM
    kern = functools.partial(_sb_attn_kernel, tq=tq, bk=bk, n_heads=heads_per_step)
    vmem = 2 * (2 * s * gw * 2 + 3 * tq * gw * 2) + 2 * tq * gw * 4 + heads_per_step * 16 * tq * bk * 4
    return pl.pallas_call(
        kern,
        grid=(n_groups, s // tq),
        in_specs=[
            pl.BlockSpec((tq, gw), lambda h, i: (i, h)),
            pl.BlockSpec((s, gw), lambda h, i: (0, h)),
            pl.BlockSpec((s, gw), lambda h, i: (0, n_groups + h)),
            pl.BlockSpec((tq, gw), lambda h, i: (i, n_groups + h)),
        ],
        out_specs=pl.BlockSpec((tq, gw), lambda h, i: (i, h)),
        out_shape=jax.ShapeDtypeStruct((s, n_groups * gw), BF16),
        scratch_shapes=[pltpu.VMEM((tq, gw), F32), pltpu.VMEM((heads_per_step, tq, 1), F32),
                        pltpu.VMEM((tq, gw), F32), pltpu.SMEM((1,), F32)],
        compiler_params=pltpu.CompilerParams(
            dimension_semantics=("arbitrary", "arbitrary"),
            vmem_limit_bytes=_vmem_limit(vmem)),
        name="sb_attention",
    )(qg, kv, kv, qg)


def kernel(x, a_pre_norm, a_w_in, a_w_group, a_scale, a_w_out, a_post_norm, kv_norm, w_kv,
           b_pre_norm, b_w_in, b_w_out, b_post_norm):
    b, s, d = x.shape
    assert b == 1
    n_a, n_b = a_w_in.shape[0], b_w_in.shape[0]
    w_kv = w_kv[None]

    xs = x.reshape(s, d)
    h = _norm(xs, a_pre_norm[0])
    for layer in range(n_a):
        if layer == 0:
            ug, a_w_out = _matmul(h, a_w_in, layer, also_cast=a_w_out)
        else:
            ug = _matmul(h, a_w_in, layer)
        mixed = _pool_mix(ug, a_w_group, a_scale[layer], layer)
        g_next = a_pre_norm[layer + 1] if layer + 1 < n_a else b_pre_norm[0]
        xs, h = _out_proj(mixed, a_w_out, layer, xs, a_post_norm[layer], g_next)
    kv, b_w_out = _matmul(_norm(xs, kv_norm), w_kv, 0, also_cast=b_w_out)
    for layer in range(n_b):
        qg = _matmul(h, b_w_in, layer, q_scale=LOG2_E / math.sqrt(HEAD_DIM))
        og = _sb_attention(qg, kv)
        g_next = b_pre_norm[layer + 1] if layer + 1 < n_b else None
        xs, h = _out_proj(og, b_w_out, layer, xs, b_post_norm[layer], g_next)
    return xs.reshape(b, s, d)
```

```python
import functools
import math

import jax
import jax.numpy as jnp
from jax import lax
from jax.experimental import pallas as pl
from jax.experimental.pallas import tpu as pltpu

RMS_EPS = 1e-6
HEAD_DIM = 128

V7X_VMEM_BYTES = 64 * 1024 * 1024

BF16 = jnp.bfloat16
F32 = jnp.float32
NORM_ROW_CHUNK = 32


def _vmem_limit(nbytes):
    return int(min(V7X_VMEM_BYTES - 6 * 1024 * 1024, nbytes + 8 * 1024 * 1024))


def _rms_scale(x):
    return lax.rsqrt(jnp.mean(x * x, axis=-1, keepdims=True) + RMS_EPS)


RING_DEPTH = 3


def _weight_slab(w_hbm, ring_ref, sem, layer, tn, n_rows):
    n_slabs = pl.num_programs(1)
    step = pl.program_id(0) * n_slabs + pl.program_id(1)
    n_steps = n_rows * n_slabs

    def fetch(s):
        start = (s % n_slabs) * tn
        cols = pl.ds(start if isinstance(start, int) else pl.multiple_of(start, tn), tn)
        slot = s % RING_DEPTH
        return pltpu.make_async_copy(w_hbm.at[layer, :, cols], ring_ref.at[slot], sem.at[slot])

    @pl.when(step == 0)
    def _():
        for s in range(RING_DEPTH - 1):
            fetch(s).start()

    @pl.when(step + (RING_DEPTH - 1) < n_steps)
    def _():
        fetch(step + (RING_DEPTH - 1)).start()

    fetch(step).wait()
    return ring_ref.at[step % RING_DEPTH]


def _norm_kernel(x_ref, g_ref, o_ref):
    x = x_ref[...]
    o_ref[...] = (x * _rms_scale(x) * g_ref[...]).astype(o_ref.dtype)


def _norm(x, g, *, tm=256):
    s, d = x.shape
    return pl.pallas_call(
        _norm_kernel,
        grid=(s // tm,),
        in_specs=[pl.BlockSpec((tm, d), lambda i: (i, 0)), pl.BlockSpec((1, d), lambda i: (0, 0))],
        out_specs=pl.BlockSpec((tm, d), lambda i: (i, 0)),
        out_shape=jax.ShapeDtypeStruct((s, d), BF16),
        compiler_params=pltpu.CompilerParams(dimension_semantics=("arbitrary",)),
        name="norm",
    )(x, g.reshape(1, d))


def _matmul_kernel(h_ref, w_hbm, *rest, q_tiles, q_scale, layer):
    *io_refs, ring_ref, sem = rest
    if len(io_refs) == 3:
        cast_src_ref, o_ref, cast_dst_ref = io_refs
        cast_dst_ref[...] = cast_src_ref[...].astype(cast_dst_ref.dtype)
    else:
        (o_ref,) = io_refs
    w_ref = _weight_slab(w_hbm, ring_ref, sem, layer, o_ref.shape[1], pl.num_programs(0))
    acc = jnp.dot(h_ref[...], w_ref[...].astype(h_ref.dtype), preferred_element_type=F32)
    if q_tiles:
        acc = acc * jnp.where(pl.program_id(1) < q_tiles, q_scale, 1.0).astype(F32)
    o_ref[...] = acc.astype(o_ref.dtype)


def _matmul(h, w, layer, *, q_scale=None, also_cast=None, tm=1024, tn=512):
    s, d = h.shape
    n = w.shape[2]
    n_i, n_j = s // tm, n // tn
    q_tiles = (n // 2) // tn if q_scale is not None else 0
    kern = functools.partial(_matmul_kernel, q_tiles=q_tiles, q_scale=q_scale if q_scale is not None else 1.0,
                             layer=layer)
    vmem = (2 * tm * d * 2 + RING_DEPTH * d * tn * w.dtype.itemsize + d * tn * 2 + 2 * tm * tn * 2
            + tm * tn * 4)
    operands = [h, w]
    in_specs = [
        pl.BlockSpec((tm, d), lambda i, j: (i, 0)),
        pl.BlockSpec(memory_space=pl.ANY),
    ]
    out_specs = [pl.BlockSpec((tm, tn), lambda i, j: (i, j))]
    out_shape = [jax.ShapeDtypeStruct((s, n), BF16)]
    if also_cast is not None:
        cols = also_cast.shape[-1]
        flat = also_cast.reshape(-1, cols)
        rows_per_step = flat.shape[0] // (n_i * n_j)
        assert rows_per_step * n_i * n_j == flat.shape[0]
        operands.append(flat)
        cast_spec = pl.BlockSpec((rows_per_step, cols), lambda i, j: (i * n_j + j, 0))
        in_specs.append(cast_spec)
        out_specs.append(cast_spec)
        out_shape.append(jax.ShapeDtypeStruct(flat.shape, BF16))
        vmem += 2 * rows_per_step * cols * (4 + 2)
    outs = pl.pallas_call(
        kern,
        grid=(n_i, n_j),
        in_specs=in_specs,
        out_specs=out_specs,
        out_shape=out_shape,
        scratch_shapes=[pltpu.VMEM((RING_DEPTH, d, tn), w.dtype), pltpu.SemaphoreType.DMA((RING_DEPTH,))],
        compiler_params=pltpu.CompilerParams(
            dimension_semantics=("arbitrary", "arbitrary"),
            vmem_limit_bytes=_vmem_limit(vmem)),
        name="in_proj",
    )(*operands)
    if also_cast is None:
        return outs[0]
    return outs[0], outs[1].reshape(also_cast.shape)


POOL_SUB = 128


def _pool_kernel(u_ref, halo_ref, gate_ref, wg_ref, scale_ref, o_ref, pooled_ref):
    g = pl.program_id(0)
    i = pl.program_id(1)
    tm = u_ref.shape[0]
    w = lax.shift_left(jnp.int32(2), g)

    r = lax.broadcasted_iota(jnp.int32, (POOL_SUB, POOL_SUB), 0)
    c = lax.broadcasted_iota(jnp.int32, (POOL_SUB, POOL_SUB), 1)
    d_cur = r - c
    d_prev = d_cur + POOL_SUB
    band_cur = ((d_cur >= 0) & (d_cur < w)).astype(BF16)
    band_prev = (d_prev < w).astype(BF16)
    band = jnp.concatenate([band_prev, band_cur], axis=1)

    for rb in range(tm // POOL_SUB):
        rows = slice(rb * POOL_SUB, (rb + 1) * POOL_SUB)
        cur = u_ref[rows, :]
        if rb == 0:
            halo = halo_ref[...]
            prev_and_cur = jnp.concatenate([jnp.where(i > 0, halo, jnp.zeros_like(halo)), cur], axis=0)
        else:
            prev_and_cur = u_ref[(rb - 1) * POOL_SUB:(rb + 1) * POOL_SUB, :]
        wsum = jnp.dot(band, prev_and_cur, preferred_element_type=F32)
        t = i * tm + rb * POOL_SUB + lax.broadcasted_iota(jnp.int32, (POOL_SUB, 1), 0)
        inv_count = 1.0 / jnp.minimum(t + 1, w).astype(F32)
        pooled_ref[rows, :] = (wsum * inv_count - cur.astype(F32)).astype(pooled_ref.dtype)

    mixed = jnp.dot(pooled_ref[...], wg_ref[...].astype(BF16), preferred_element_type=F32)
    gate = gate_ref[...].astype(F32)
    o_ref[...] = (mixed * scale_ref[...] * (gate * jax.nn.sigmoid(gate))).astype(o_ref.dtype)


def _pool_mix(ug, w_group, scale, layer, *, tm=512):
    s = ug.shape[0]
    _, n_groups, gd, _ = w_group.shape
    width = n_groups * gd
    sub_per_tile = tm // POOL_SUB
    vmem = (2 * (2 * tm * gd * 2 + POOL_SUB * gd * 2 + gd * gd * w_group.dtype.itemsize + tm * gd * 2)
            + tm * gd * 2 + gd * gd * 2 + 2 * tm * gd * 4)
    return pl.pallas_call(
        _pool_kernel,
        grid=(n_groups, s // tm),
        in_specs=[
            pl.BlockSpec((tm, gd), lambda g, i: (i, g)),
            pl.BlockSpec((POOL_SUB, gd), lambda g, i: (jnp.maximum(i * sub_per_tile - 1, 0), g)),
            pl.BlockSpec((tm, gd), lambda g, i: (i, n_groups + g)),
            pl.BlockSpec((None, None, gd, gd), lambda g, i: (layer, g, 0, 0)),
            pl.BlockSpec((1, gd), lambda g, i: (0, g)),
        ],
        out_specs=pl.BlockSpec((tm, gd), lambda g, i: (i, g)),
        out_shape=jax.ShapeDtypeStruct((s, width), BF16),
        scratch_shapes=[pltpu.VMEM((tm, gd), BF16)],
        compiler_params=pltpu.CompilerParams(
            dimension_semantics=("arbitrary", "arbitrary"),
            vmem_limit_bytes=_vmem_limit(vmem)),
        name="pool_mix",
    )(ug, ug, ug, w_group, scale.reshape(1, width))


def _out_proj_kernel(a_ref, w_hbm, x_ref, gpost_ref, gnext_ref, xo_ref, *rest, tn, layer, n_row_tiles, want_h):
    if want_h:
        ho_ref, work_ref, xfull_ref, hdone_ref, ring_ref, sem = rest
    else:
        work_ref, xfull_ref, ring_ref, sem = rest
    row = pl.program_id(0)
    j = pl.program_id(1)
    tm = work_ref.shape[0]
    cols = pl.ds(pl.multiple_of(j * tn, tn), tn)

    @pl.when(row > 0)
    def _():
        xo_ref[...] = work_ref[:, cols]
        if want_h:
            ho_ref[...] = hdone_ref[:, cols]

    @pl.when(row < n_row_tiles)
    def _():
        w_ref = _weight_slab(w_hbm, ring_ref, sem, layer, tn, n_row_tiles)
        work_ref[:, cols] = jnp.dot(a_ref[...], w_ref[...], preferred_element_type=F32)
        xfull_ref[:, cols] = x_ref[...]

    @pl.when((row < n_row_tiles) & (j == pl.num_programs(1) - 1))
    def _():
        gpost = gpost_ref[...]
        gnext = gnext_ref[...]
        for c in range(tm // NORM_ROW_CHUNK):
            r = pl.ds(c * NORM_ROW_CHUNK, NORM_ROW_CHUNK)
            y = work_ref[r, :]
            xn = xfull_ref[r, :] + y * _rms_scale(y) * gpost
            work_ref[r, :] = xn
            if want_h:
                hdone_ref[r, :] = (xn * _rms_scale(xn) * gnext).astype(hdone_ref.dtype)


def _out_proj(a, w, layer, x, g_post, g_next=None, *, tm=512, tn=512):
    s, kdim = a.shape
    d = w.shape[2]
    n_row_tiles, n_slabs = s // tm, d // tn
    want_h = g_next is not None
    kern = functools.partial(_out_proj_kernel, tn=tn, layer=layer, n_row_tiles=n_row_tiles, want_h=want_h)

    def in_row(i, j):
        return jnp.minimum(i, n_row_tiles - 1)

    def in_slab(i, j):
        return jnp.where(i < n_row_tiles, j, n_slabs - 1)

    def out_block(i, j):
        return jnp.maximum(i - 1, 0), jnp.where(i == 0, 0, j)

    out_dtypes = [F32, BF16] if want_h else [F32]
    vmem = (2 * tm * kdim * 2 + RING_DEPTH * kdim * tn * 2 + 2 * tm * tn * 4 + 2 * tm * d * 4
            + (tm * d * 2 if want_h else 0) + sum(2 * tm * tn * jnp.dtype(t).itemsize for t in out_dtypes)
            + tm * tn * 4)
    outs = pl.pallas_call(
        kern,
        grid=(n_row_tiles + 1, n_slabs),
        in_specs=[
            pl.BlockSpec((tm, kdim), lambda i, j: (in_row(i, j), 0)),
            pl.BlockSpec(memory_space=pl.ANY),
            pl.BlockSpec((tm, tn), lambda i, j: (in_row(i, j), in_slab(i, j))),
            pl.BlockSpec((1, d), lambda i, j: (0, 0)),
            pl.BlockSpec((1, d), lambda i, j: (0, 0)),
        ],
        out_specs=[pl.BlockSpec((tm, tn), out_block) for _ in out_dtypes],
        out_shape=[jax.ShapeDtypeStruct((s, d), t) for t in out_dtypes],
        scratch_shapes=([pltpu.VMEM((tm, d), F32), pltpu.VMEM((tm, d), F32)]
                        + ([pltpu.VMEM((tm, d), BF16)] if want_h else [])
                        + [pltpu.VMEM((RING_DEPTH, kdim, tn), w.dtype), pltpu.SemaphoreType.DMA((RING_DEPTH,))]),
        compiler_params=pltpu.CompilerParams(
            dimension_semantics=("arbitrary", "arbitrary"),
            vmem_limit_bytes=_vmem_limit(vmem)),
        name="out_proj",
    )(a, w, x, g_post.reshape(1, d), (g_next if want_h else g_post).reshape(1, d))
    return (outs[0], outs[1]) if want_h else (outs[0], None)


LOG2_E = math.log2(math.e)
SOFTPLUS2_LINEAR_ABOVE = 64.0
MASKED_EXPONENT = -1e30
SKIP_CARRY = 160.0


def _softplus2(z):
    return jnp.where(z > SOFTPLUS2_LINEAR_ABOVE, z, jnp.log(1.0 + jnp.exp2(z)) * LOG2_E)


def _sb_attn_kernel(q_ref, k_ref, v_ref, gate_ref, o_ref, acc_ref, carry_ref, sgate_ref, low_ref,
                    *, tq, bk, n_heads):
    i = pl.program_id(1)
    top = 2 * i + 1
    r = lax.broadcasted_iota(jnp.int32, (bk, bk), 0)
    c = lax.broadcasted_iota(jnp.int32, (bk, bk), 1)
    suffix = (r >= c).astype(BF16)
    before = c < r

    def key_rows(kb):
        return pl.ds(pl.multiple_of(kb * bk, bk), bk)

    def head_cols(head):
        return pl.ds(head * HEAD_DIM, HEAD_DIM)

    def scores(head, q_rows, kb, diagonal):
        z = lax.dot_general(q_ref[q_rows, head_cols(head)], k_ref[key_rows(kb), head_cols(head)],
                            (((1,), (1,)), ((), ())), preferred_element_type=F32)
        sp = _softplus2(z)
        if diagonal:
            sp = jnp.where(before, sp, 0.0)
        s_in = jnp.dot(sp.astype(BF16), suffix, preferred_element_type=F32)
        expo = jnp.minimum(z - s_in, 0.0)
        if diagonal:
            expo = jnp.where(before, expo, MASKED_EXPONENT)
        return expo, s_in[:, 0:1]

    def accumulate(head, q_rows, kb, expo, tot):
        carry = carry_ref[head, q_rows, :]
        a = jnp.exp2(expo - carry)
        acc_ref[q_rows, head_cols(head)] += jnp.dot(a.astype(BF16), v_ref[key_rows(kb), head_cols(head)],
                                                    preferred_element_type=F32)
        carry_ref[head, q_rows, :] = carry + tot

    def run_blocks(blocks):
        work = [(head,) + blk for blk in blocks for head in range(n_heads)]
        parts = [scores(*item) for item in work]
        for (head, q_rows, kb, _), (expo, tot) in zip(work, parts):
            accumulate(head, q_rows, kb, expo, tot)

    def min_carry():
        return jnp.min(carry_ref[...])

    acc_ref[...] = jnp.zeros_like(acc_ref)
    carry_ref[...] = jnp.zeros_like(carry_ref)
    early_rows = pl.ds(0, bk)
    late_rows = pl.ds(bk, bk)
    all_rows = pl.ds(0, tq)

    def first_blocks(blocks):
        run_blocks(blocks)
        low_ref[0] = min_carry()
        gate = gate_ref[...].astype(F32)
        sgate_ref[...] = gate * jax.nn.sigmoid(gate)

    @pl.when(i == 0)
    def _():
        first_blocks([(late_rows, top, True), (early_rows, top - 1, True), (late_rows, top - 1, False)])

    @pl.when(i > 0)
    def _():
        first_blocks([(late_rows, top, True), (early_rows, top - 1, True),
                      (late_rows, top - 1, False), (early_rows, top - 2, False)])

    @pl.when((i > 0) & (low_ref[0] < SKIP_CARRY))
    def _():
        run_blocks([(late_rows, top - 2, False)])
        low_ref[0] = min_carry()

    def keep_going(state):
        kb, lowest = state
        return (kb >= 0) & (lowest < SKIP_CARRY)

    def body(state):
        kb, _ = state
        run_blocks([(all_rows, kb, False)])
        return kb - 1, min_carry()

    lax.while_loop(keep_going, body, (top - 3, low_ref[0]))
    o_ref[...] = (acc_ref[...] * sgate_ref[...]).astype(o_ref.dtype)


def _sb_attention(qg, kv, *, bk=256, heads_per_step=4):
    s, two_w = qg.shape
    n_groups = (two_w // 2) // (heads_per_step * HEAD_DIM)
    tq = 2 * bk
    gw = heads_per_step * HEAD_DIM
    kern = functools.partial(_sb_attn_kernel, tq=tq, bk=bk, n_heads=heads_per_step)
    vmem = 2 * (2 * s * gw * 2 + 3 * tq * gw * 2) + 2 * tq * gw * 4 + heads_per_step * 16 * tq * bk * 4
    return pl.pallas_call(
        kern,
        grid=(n_groups, s // tq),
        in_specs=[
            pl.BlockSpec((tq, gw), lambda h, i: (i, h)),
            pl.BlockSpec((s, gw), lambda h, i: (0, h)),
            pl.BlockSpec((s, gw), lambda h, i: (0, n_groups + h)),
            pl.BlockSpec((tq, gw), lambda h, i: (i, n_groups + h)),
        ],
        out_specs=pl.BlockSpec((tq, gw), lambda h, i: (i, h)),
        out_shape=jax.ShapeDtypeStruct((s, n_groups * gw), BF16),
        scratch_shapes=[pltpu.VMEM((tq, gw), F32), pltpu.VMEM((heads_per_step, tq, 1), F32),
                        pltpu.VMEM((tq, gw), F32), pltpu.SMEM((1,), F32)],
        compiler_params=pltpu.CompilerParams(
            dimension_semantics=("arbitrary", "arbitrary"),
            vmem_limit_bytes=_vmem_limit(vmem)),
        name="sb_attention",
    )(qg, kv, kv, qg)


def kernel(x, a_pre_norm, a_w_in, a_w_group, a_scale, a_w_out, a_post_norm, kv_norm, w_kv,
           b_pre_norm, b_w_in, b_w_out, b_post_norm):
    b, s, d = x.shape
    assert b == 1
    n_a, n_b = a_w_in.shape[0], b_w_in.shape[0]
    w_kv = w_kv[None]

    xs = x.reshape(s, d)
    h = _norm(xs, a_pre_norm[0])
    for layer in range(n_a):
        if layer == 0:
            ug, a_w_out = _matmul(h, a_w_in, layer, also_cast=a_w_out)
        else:
            ug = _matmul(h, a_w_in, layer)
        mixed = _pool_mix(ug, a_w_group, a_scale[layer], layer)
        g_next = a_pre_norm[layer + 1] if layer + 1 < n_a else b_pre_norm[0]
        xs, h = _out_proj(mixed, a_w_out, layer, xs, a_post_norm[layer], g_next)
    kv, b_w_out = _matmul(_norm(xs, kv_norm), w_kv, 0, also_cast=b_w_out)
    for layer in range(n_b):
        qg = _matmul(h, b_w_in, layer, q_scale=LOG2_E / math.sqrt(HEAD_DIM))
        og = _sb_attention(qg, kv)
        g_next = b_pre_norm[layer + 1] if layer + 1 < n_b else None
        xs, h = _out_proj(og, b_w_out, layer, xs, b_post_norm[layer], g_next)
    return xs.reshape(b, s, d)
```

```python
import functools
import math

import jax
import jax.numpy as jnp
from jax import lax
from jax.experimental import pallas as pl
from jax.experimental.pallas import tpu as pltpu

RMS_EPS = 1e-6
HEAD_DIM = 128

V7X_VMEM_BYTES = 64 * 1024 * 1024

BF16 = jnp.bfloat16
F32 = jnp.float32
NORM_ROW_CHUNK = 32


def _vmem_limit(nbytes):
    return int(min(V7X_VMEM_BYTES - 6 * 1024 * 1024, nbytes + 8 * 1024 * 1024))


def _rms_scale(x):
    return lax.rsqrt(jnp.mean(x * x, axis=-1, keepdims=True) + RMS_EPS)


RING_DEPTH = 3


def _weight_slab(w_hbm, ring_ref, sem, layer, tn, n_rows):
    n_slabs = pl.num_programs(1)
    step = pl.program_id(0) * n_slabs + pl.program_id(1)
    n_steps = n_rows * n_slabs

    def fetch(s):
        start = (s % n_slabs) * tn
        cols = pl.ds(start if isinstance(start, int) else pl.multiple_of(start, tn), tn)
        slot = s % RING_DEPTH
        return pltpu.make_async_copy(w_hbm.at[layer, :, cols], ring_ref.at[slot], sem.at[slot])

    @pl.when(step == 0)
    def _():
        for s in range(RING_DEPTH - 1):
            fetch(s).start()

    @pl.when(step + (RING_DEPTH - 1) < n_steps)
    def _():
        fetch(step + (RING_DEPTH - 1)).start()

    fetch(step).wait()
    return ring_ref.at[step % RING_DEPTH]


def _norm_kernel(x_ref, g_ref, o_ref):
    x = x_ref[...]
    o_ref[...] = (x * _rms_scale(x) * g_ref[...]).astype(o_ref.dtype)


def _norm(x, g, *, tm=256):
    s, d = x.shape
    return pl.pallas_call(
        _norm_kernel,
        grid=(s // tm,),
        in_specs=[pl.BlockSpec((tm, d), lambda i: (i, 0)), pl.BlockSpec((1, d), lambda i: (0, 0))],
        out_specs=pl.BlockSpec((tm, d), lambda i: (i, 0)),
        out_shape=jax.ShapeDtypeStruct((s, d), BF16),
        compiler_params=pltpu.CompilerParams(dimension_semantics=("arbitrary",)),
        name="norm",
    )(x, g.reshape(1, d))


def _matmul_kernel(h_ref, w_hbm, *rest, q_tiles, q_scale, layer):
    *io_refs, ring_ref, sem = rest
    if len(io_refs) == 3:
        cast_src_ref, o_ref, cast_dst_ref = io_refs
        cast_dst_ref[...] = cast_src_ref[...].astype(cast_dst_ref.dtype)
    else:
        (o_ref,) = io_refs
    w_ref = _weight_slab(w_hbm, ring_ref, sem, layer, o_ref.shape[1], pl.num_programs(0))
    acc = jnp.dot(h_ref[...], w_ref[...].astype(h_ref.dtype), preferred_element_type=F32)
    if q_tiles:
        acc = acc * jnp.where(pl.program_id(1) < q_tiles, q_scale, 1.0).astype(F32)
    o_ref[...] = acc.astype(o_ref.dtype)


def _matmul(h, w, layer, *, q_scale=None, also_cast=None, tm=1024, tn=512):
    s, d = h.shape
    n = w.shape[2]
    n_i, n_j = s // tm, n // tn
    q_tiles = (n // 2) // tn if q_scale is not None else 0
    kern = functools.partial(_matmul_kernel, q_tiles=q_tiles, q_scale=q_scale if q_scale is not None else 1.0,
                             layer=layer)
    vmem = (2 * tm * d * 2 + RING_DEPTH * d * tn * w.dtype.itemsize + d * tn * 2 + 2 * tm * tn * 2
            + tm * tn * 4)
    operands = [h, w]
    in_specs = [
        pl.BlockSpec((tm, d), lambda i, j: (i, 0)),
        pl.BlockSpec(memory_space=pl.ANY),
    ]
    out_specs = [pl.BlockSpec((tm, tn), lambda i, j: (i, j))]
    out_shape = [jax.ShapeDtypeStruct((s, n), BF16)]
    if also_cast is not None:
        cols = also_cast.shape[-1]
        flat = also_cast.reshape(-1, cols)
        rows_per_step = flat.shape[0] // (n_i * n_j)
        assert rows_per_step * n_i * n_j == flat.shape[0]
        operands.append(flat)
        cast_spec = pl.BlockSpec((rows_per_step, cols), lambda i, j: (i * n_j + j, 0))
        in_specs.append(cast_spec)
        out_specs.append(cast_spec)
        out_shape.append(jax.ShapeDtypeStruct(flat.shape, BF16))
        vmem += 2 * rows_per_step * cols * (4 + 2)
    outs = pl.pallas_call(
        kern,
        grid=(n_i, n_j),
        in_specs=in_specs,
        out_specs=out_specs,
        out_shape=out_shape,
        scratch_shapes=[pltpu.VMEM((RING_DEPTH, d, tn), w.dtype), pltpu.SemaphoreType.DMA((RING_DEPTH,))],
        compiler_params=pltpu.CompilerParams(
            dimension_semantics=("arbitrary", "arbitrary"),
            vmem_limit_bytes=_vmem_limit(vmem)),
        name="in_proj",
    )(*operands)
    if also_cast is None:
        return outs[0]
    return outs[0], outs[1].reshape(also_cast.shape)


POOL_SUB = 128


def _pool_kernel(u_ref, halo_ref, gate_ref, wg_ref, scale_ref, o_ref, pooled_ref):
    g = pl.program_id(0)
    i = pl.program_id(1)
    tm = u_ref.shape[0]
    w = lax.shift_left(jnp.int32(2), g)

    r = lax.broadcasted_iota(jnp.int32, (POOL_SUB, POOL_SUB), 0)
    c = lax.broadcasted_iota(jnp.int32, (POOL_SUB, POOL_SUB), 1)
    d_cur = r - c
    d_prev = d_cur + POOL_SUB
    band_cur = ((d_cur >= 0) & (d_cur < w)).astype(BF16)
    band_prev = (d_prev < w).astype(BF16)
    band = jnp.concatenate([band_prev, band_cur], axis=1)

    for rb in range(tm // POOL_SUB):
        rows = slice(rb * POOL_SUB, (rb + 1) * POOL_SUB)
        cur = u_ref[rows, :]
        if rb == 0:
            halo = halo_ref[...]
            prev_and_cur = jnp.concatenate([jnp.where(i > 0, halo, jnp.zeros_like(halo)), cur], axis=0)
        else:
            prev_and_cur = u_ref[(rb - 1) * POOL_SUB:(rb + 1) * POOL_SUB, :]
        wsum = jnp.dot(band, prev_and_cur, preferred_element_type=F32)
        t = i * tm + rb * POOL_SUB + lax.broadcasted_iota(jnp.int32, (POOL_SUB, 1), 0)
        inv_count = 1.0 / jnp.minimum(t + 1, w).astype(F32)
        pooled_ref[rows, :] = (wsum * inv_count - cur.astype(F32)).astype(pooled_ref.dtype)

    mixed = jnp.dot(pooled_ref[...], wg_ref[...].astype(BF16), preferred_element_type=F32)
    gate = gate_ref[...].astype(F32)
    o_ref[...] = (mixed * scale_ref[...] * (gate * jax.nn.sigmoid(gate))).astype(o_ref.dtype)


def _pool_mix(ug, w_group, scale, layer, *, tm=512):
    s = ug.shape[0]
    _, n_groups, gd, _ = w_group.shape
    width = n_groups * gd
    sub_per_tile = tm // POOL_SUB
    vmem = (2 * (2 * tm * gd * 2 + POOL_SUB * gd * 2 + gd * gd * w_group.dtype.itemsize + tm * gd * 2)
            + tm * gd * 2 + gd * gd * 2 + 2 * tm * gd * 4)
    return pl.pallas_call(
        _pool_kernel,
        grid=(n_groups, s // tm),
        in_specs=[
            pl.BlockSpec((tm, gd), lambda g, i: (i, g)),
            pl.BlockSpec((POOL_SUB, gd), lambda g, i: (jnp.maximum(i * sub_per_tile - 1, 0), g)),
            pl.BlockSpec((tm, gd), lambda g, i: (i, n_groups + g)),
            pl.BlockSpec((None, None, gd, gd), lambda g, i: (layer, g, 0, 0)),
            pl.BlockSpec((1, gd), lambda g, i: (0, g)),
        ],
        out_specs=pl.BlockSpec((tm, gd), lambda g, i: (i, g)),
        out_shape=jax.ShapeDtypeStruct((s, width), BF16),
        scratch_shapes=[pltpu.VMEM((tm, gd), BF16)],
        compiler_params=pltpu.CompilerParams(
            dimension_semantics=("arbitrary", "arbitrary"),
            vmem_limit_bytes=_vmem_limit(vmem)),
        name="pool_mix",
    )(ug, ug, ug, w_group, scale.reshape(1, width))


def _out_proj_kernel(a_ref, w_hbm, x_ref, gpost_ref, *rest, tn, layer, n_row_tiles, n_h):
    gnext_refs = rest[:n_h]
    xo_ref = rest[n_h]
    ho_refs = rest[n_h + 1:2 * n_h + 1]
    work_ref, xfull_ref = rest[2 * n_h + 1:2 * n_h + 3]
    hdone_refs = rest[2 * n_h + 3:3 * n_h + 3]
    ring_ref, sem = rest[3 * n_h + 3:]
    row = pl.program_id(0)
    j = pl.program_id(1)
    tm = work_ref.shape[0]
    cols = pl.ds(pl.multiple_of(j * tn, tn), tn)

    @pl.when(row > 0)
    def _():
        xo_ref[...] = work_ref[:, cols]
        for ho_ref, hdone_ref in zip(ho_refs, hdone_refs):
            ho_ref[...] = hdone_ref[:, cols]

    @pl.when(row < n_row_tiles)
    def _():
        w_ref = _weight_slab(w_hbm, ring_ref, sem, layer, tn, n_row_tiles)
        work_ref[:, cols] = jnp.dot(a_ref[...], w_ref[...], preferred_element_type=F32)
        xfull_ref[:, cols] = x_ref[...]

    @pl.when((row < n_row_tiles) & (j == pl.num_programs(1) - 1))
    def _():
        gpost = gpost_ref[...]
        gnexts = [g_ref[...] for g_ref in gnext_refs]
        for c in range(tm // NORM_ROW_CHUNK):
            r = pl.ds(c * NORM_ROW_CHUNK, NORM_ROW_CHUNK)
            y = work_ref[r, :]
            xn = xfull_ref[r, :] + y * _rms_scale(y) * gpost
            work_ref[r, :] = xn
            if n_h:
                normed = xn * _rms_scale(xn)
                for gnext, hdone_ref in zip(gnexts, hdone_refs):
                    hdone_ref[r, :] = (normed * gnext).astype(hdone_ref.dtype)


def _out_proj(a, w, layer, x, g_post, g_next=(), *, tm=512, tn=512):
    s, kdim = a.shape
    d = w.shape[2]
    n_row_tiles, n_slabs = s // tm, d // tn
    n_h = len(g_next)
    kern = functools.partial(_out_proj_kernel, tn=tn, layer=layer, n_row_tiles=n_row_tiles, n_h=n_h)

    def in_row(i, j):
        return jnp.minimum(i, n_row_tiles - 1)

    def in_slab(i, j):
        return jnp.where(i < n_row_tiles, j, n_slabs - 1)

    def out_block(i, j):
        return jnp.maximum(i - 1, 0), jnp.where(i == 0, 0, j)

    out_dtypes = [F32] + [BF16] * n_h
    gain_spec = pl.BlockSpec((1, d), lambda i, j: (0, 0))
    vmem = (2 * tm * kdim * 2 + RING_DEPTH * kdim * tn * 2 + 2 * tm * tn * 4 + 2 * tm * d * 4
            + n_h * tm * d * 2 + sum(2 * tm * tn * jnp.dtype(t).itemsize for t in out_dtypes) + tm * tn * 4)
    outs = pl.pallas_call(
        kern,
        grid=(n_row_tiles + 1, n_slabs),
        in_specs=[
            pl.BlockSpec((tm, kdim), lambda i, j: (in_row(i, j), 0)),
            pl.BlockSpec(memory_space=pl.ANY),
            pl.BlockSpec((tm, tn), lambda i, j: (in_row(i, j), in_slab(i, j))),
            gain_spec,
        ] + [gain_spec] * n_h,
        out_specs=[pl.BlockSpec((tm, tn), out_block) for _ in out_dtypes],
        out_shape=[jax.ShapeDtypeStruct((s, d), t) for t in out_dtypes],
        scratch_shapes=([pltpu.VMEM((tm, d), F32), pltpu.VMEM((tm, d), F32)]
                        + [pltpu.VMEM((tm, d), BF16)] * n_h
                        + [pltpu.VMEM((RING_DEPTH, kdim, tn), w.dtype), pltpu.SemaphoreType.DMA((RING_DEPTH,))]),
        compiler_params=pltpu.CompilerParams(
            dimension_semantics=("arbitrary", "arbitrary"),
            vmem_limit_bytes=_vmem_limit(vmem)),
        name="out_proj",
    )(a, w, x, g_post.reshape(1, d), *[g.reshape(1, d) for g in g_next])
    return outs[0], list(outs[1:])


LOG2_E = math.log2(math.e)
SOFTPLUS2_LINEAR_ABOVE = 64.0
MASKED_EXPONENT = -1e30
SKIP_CARRY = 160.0


def _softplus2(z):
    return jnp.where(z > SOFTPLUS2_LINEAR_ABOVE, z, jnp.log(1.0 + jnp.exp2(z)) * LOG2_E)


def _sb_attn_kernel(q_ref, k_ref, v_ref, gate_ref, o_ref, acc_ref, carry_ref, sgate_ref, low_ref,
                    *, tq, bk, n_heads):
    i = pl.program_id(1)
    top = 2 * i + 1
    r = lax.broadcasted_iota(jnp.int32, (bk, bk), 0)
    c = lax.broadcasted_iota(jnp.int32, (bk, bk), 1)
    suffix = (r >= c).astype(BF16)
    before = c < r

    def key_rows(kb):
        return pl.ds(pl.multiple_of(kb * bk, bk), bk)

    def head_cols(head):
        return pl.ds(head * HEAD_DIM, HEAD_DIM)

    def scores(head, q_rows, kb, diagonal):
        z = lax.dot_general(q_ref[q_rows, head_cols(head)], k_ref[key_rows(kb), head_cols(head)],
                            (((1,), (1,)), ((), ())), preferred_element_type=F32)
        sp = _softplus2(z)
        if diagonal:
            sp = jnp.where(before, sp, 0.0)
        s_in = jnp.dot(sp.astype(BF16), suffix, preferred_element_type=F32)
        expo = jnp.minimum(z - s_in, 0.0)
        if diagonal:
            expo = jnp.where(before, expo, MASKED_EXPONENT)
        return expo, s_in[:, 0:1]

    def accumulate(head, q_rows, kb, expo, tot):
        carry = carry_ref[head, q_rows, :]
        a = jnp.exp2(expo - carry)
        acc_ref[q_rows, head_cols(head)] += jnp.dot(a.astype(BF16), v_ref[key_rows(kb), head_cols(head)],
                                                    preferred_element_type=F32)
        carry_ref[head, q_rows, :] = carry + tot

    def run_blocks(blocks):
        work = [(head,) + blk for blk in blocks for head in range(n_heads)]
        parts = [scores(*item) for item in work]
        for (head, q_rows, kb, _), (expo, tot) in zip(work, parts):
            accumulate(head, q_rows, kb, expo, tot)

    def min_carry():
        return jnp.min(carry_ref[...])

    acc_ref[...] = jnp.zeros_like(acc_ref)
    carry_ref[...] = jnp.zeros_like(carry_ref)
    early_rows = pl.ds(0, bk)
    late_rows = pl.ds(bk, bk)
    all_rows = pl.ds(0, tq)

    def first_blocks(blocks):
        run_blocks(blocks)
        low_ref[0] = min_carry()
        gate = gate_ref[...].astype(F32)
        sgate_ref[...] = gate * jax.nn.sigmoid(gate)

    @pl.when(i == 0)
    def _():
        first_blocks([(late_rows, top, True), (early_rows, top - 1, True), (late_rows, top - 1, False)])

    @pl.when(i > 0)
    def _():
        first_blocks([(late_rows, top, True), (early_rows, top - 1, True),
                      (late_rows, top - 1, False), (early_rows, top - 2, False)])

    @pl.when((i > 0) & (low_ref[0] < SKIP_CARRY))
    def _():
        run_blocks([(late_rows, top - 2, False)])
        low_ref[0] = min_carry()

    def keep_going(state):
        kb, lowest = state
        return (kb >= 0) & (lowest < SKIP_CARRY)

    def body(state):
        kb, _ = state
        run_blocks([(all_rows, kb, False)])
        return kb - 1, min_carry()

    lax.while_loop(keep_going, body, (top - 3, low_ref[0]))
    o_ref[...] = (acc_ref[...] * sgate_ref[...]).astype(o_ref.dtype)


def _sb_attention(qg, kv, *, bk=256, heads_per_step=4):
    s, two_w = qg.shape
    n_groups = (two_w // 2) // (heads_per_step * HEAD_DIM)
    tq = 2 * bk
    gw = heads_per_step * HEAD_DIM
    kern = functools.partial(_sb_attn_kernel, tq=tq, bk=bk, n_heads=heads_per_step)
    vmem = 2 * (2 * s * gw * 2 + 3 * tq * gw * 2) + 2 * tq * gw * 4 + heads_per_step * 16 * tq * bk * 4
    return pl.pallas_call(
        kern,
        grid=(n_groups, s // tq),
        in_specs=[
            pl.BlockSpec((tq, gw), lambda h, i: (i, h)),
            pl.BlockSpec((s, gw), lambda h, i: (0, h)),
            pl.BlockSpec((s, gw), lambda h, i: (0, n_groups + h)),
            pl.BlockSpec((tq, gw), lambda h, i: (i, n_groups + h)),
        ],
        out_specs=pl.BlockSpec((tq, gw), lambda h, i: (i, h)),
        out_shape=jax.ShapeDtypeStruct((s, n_groups * gw), BF16),
        scratch_shapes=[pltpu.VMEM((tq, gw), F32), pltpu.VMEM((heads_per_step, tq, 1), F32),
                        pltpu.VMEM((tq, gw), F32), pltpu.SMEM((1,), F32)],
        compiler_params=pltpu.CompilerParams(
            dimension_semantics=("arbitrary", "arbitrary"),
            vmem_limit_bytes=_vmem_limit(vmem)),
        name="sb_attention",
    )(qg, kv, kv, qg)


def kernel(x, a_pre_norm, a_w_in, a_w_group, a_scale, a_w_out, a_post_norm, kv_norm, w_kv,
           b_pre_norm, b_w_in, b_w_out, b_post_norm):
    b, s, d = x.shape
    assert b == 1
    n_a, n_b = a_w_in.shape[0], b_w_in.shape[0]
    w_kv = w_kv[None]

    xs = x.reshape(s, d)
    h = _norm(xs, a_pre_norm[0])
    for layer in range(n_a):
        if layer == 0:
            ug, a_w_out = _matmul(h, a_w_in, layer, also_cast=a_w_out)
        else:
            ug = _matmul(h, a_w_in, layer)
        mixed = _pool_mix(ug, a_w_group, a_scale[layer], layer)
        if layer + 1 < n_a:
            xs, (h,) = _out_proj(mixed, a_w_out, layer, xs, a_post_norm[layer], (a_pre_norm[layer + 1],))
        else:
            xs, (h, h_kv) = _out_proj(mixed, a_w_out, layer, xs, a_post_norm[layer], (b_pre_norm[0], kv_norm))
    kv, b_w_out = _matmul(h_kv, w_kv, 0, also_cast=b_w_out)
    for layer in range(n_b):
        qg = _matmul(h, b_w_in, layer, q_scale=LOG2_E / math.sqrt(HEAD_DIM))
        og = _sb_attention(qg, kv)
        g_next = (b_pre_norm[layer + 1],) if layer + 1 < n_b else ()
        xs, hs = _out_proj(og, b_w_out, layer, xs, b_post_norm[layer], g_next)
        h = hs[0] if hs else None
    return xs.reshape(b, s, d)
```

```python
import functools
import math

import jax
import jax.numpy as jnp
from jax import lax
from jax.experimental import pallas as pl
from jax.experimental.pallas import tpu as pltpu

RMS_EPS = 1e-6
HEAD_DIM = 128

V7X_VMEM_BYTES = 64 * 1024 * 1024

BF16 = jnp.bfloat16
F32 = jnp.float32
NORM_ROW_CHUNK = 32


def _vmem_limit(nbytes):
    return int(min(V7X_VMEM_BYTES - 6 * 1024 * 1024, nbytes + 8 * 1024 * 1024))


def _rms_scale(x):
    return lax.rsqrt(jnp.mean(x * x, axis=-1, keepdims=True) + RMS_EPS)


RING_DEPTH = 3


def _weight_slab(w_hbm, ring_ref, sem, layer, tn, n_rows):
    n_slabs = pl.num_programs(1)
    step = pl.program_id(0) * n_slabs + pl.program_id(1)
    n_steps = n_rows * n_slabs

    def fetch(s):
        start = (s % n_slabs) * tn
        cols = pl.ds(start if isinstance(start, int) else pl.multiple_of(start, tn), tn)
        slot = s % RING_DEPTH
        return pltpu.make_async_copy(w_hbm.at[layer, :, cols], ring_ref.at[slot], sem.at[slot])

    @pl.when(step == 0)
    def _():
        for s in range(RING_DEPTH - 1):
            fetch(s).start()

    @pl.when(step + (RING_DEPTH - 1) < n_steps)
    def _():
        fetch(step + (RING_DEPTH - 1)).start()

    fetch(step).wait()
    return ring_ref.at[step % RING_DEPTH]


def _norm_kernel(x_ref, g_ref, o_ref):
    x = x_ref[...]
    o_ref[...] = (x * _rms_scale(x) * g_ref[...]).astype(o_ref.dtype)


def _norm(x, g, *, tm=256):
    s, d = x.shape
    return pl.pallas_call(
        _norm_kernel,
        grid=(s // tm,),
        in_specs=[pl.BlockSpec((tm, d), lambda i: (i, 0)), pl.BlockSpec((1, d), lambda i: (0, 0))],
        out_specs=pl.BlockSpec((tm, d), lambda i: (i, 0)),
        out_shape=jax.ShapeDtypeStruct((s, d), BF16),
        compiler_params=pltpu.CompilerParams(dimension_semantics=("arbitrary",)),
        name="norm",
    )(x, g.reshape(1, d))


def _matmul_kernel(h_ref, w_hbm, *rest, q_tiles, q_scale, layer):
    *io_refs, ring_ref, sem = rest
    if len(io_refs) == 3:
        cast_src_ref, o_ref, cast_dst_ref = io_refs
        cast_dst_ref[...] = cast_src_ref[...].astype(cast_dst_ref.dtype)
    else:
        (o_ref,) = io_refs
    w_ref = _weight_slab(w_hbm, ring_ref, sem, layer, o_ref.shape[1], pl.num_programs(0))
    acc = jnp.dot(h_ref[...], w_ref[...].astype(h_ref.dtype), preferred_element_type=F32)
    if q_tiles:
        acc = acc * jnp.where(pl.program_id(1) < q_tiles, q_scale, 1.0).astype(F32)
    o_ref[...] = acc.astype(o_ref.dtype)


def _matmul(h, w, layer, *, q_scale=None, also_cast=None, tm=1024, tn=512):
    s, d = h.shape
    n = w.shape[2]
    n_i, n_j = s // tm, n // tn
    q_tiles = (n // 2) // tn if q_scale is not None else 0
    kern = functools.partial(_matmul_kernel, q_tiles=q_tiles, q_scale=q_scale if q_scale is not None else 1.0,
                             layer=layer)
    vmem = (2 * tm * d * 2 + RING_DEPTH * d * tn * w.dtype.itemsize + d * tn * 2 + 2 * tm * tn * 2
            + tm * tn * 4)
    operands = [h, w]
    in_specs = [
        pl.BlockSpec((tm, d), lambda i, j: (i, 0)),
        pl.BlockSpec(memory_space=pl.ANY),
    ]
    out_specs = [pl.BlockSpec((tm, tn), lambda i, j: (i, j))]
    out_shape = [jax.ShapeDtypeStruct((s, n), BF16)]
    if also_cast is not None:
        cols = also_cast.shape[-1]
        flat = also_cast.reshape(-1, cols)
        rows_per_step = flat.shape[0] // (n_i * n_j)
        assert rows_per_step * n_i * n_j == flat.shape[0]
        operands.append(flat)
        cast_spec = pl.BlockSpec((rows_per_step, cols), lambda i, j: (i * n_j + j, 0))
        in_specs.append(cast_spec)
        out_specs.append(cast_spec)
        out_shape.append(jax.ShapeDtypeStruct(flat.shape, BF16))
        vmem += 2 * rows_per_step * cols * (4 + 2)
    outs = pl.pallas_call(
        kern,
        grid=(n_i, n_j),
        in_specs=in_specs,
        out_specs=out_specs,
        out_shape=out_shape,
        scratch_shapes=[pltpu.VMEM((RING_DEPTH, d, tn), w.dtype), pltpu.SemaphoreType.DMA((RING_DEPTH,))],
        compiler_params=pltpu.CompilerParams(
            dimension_semantics=("arbitrary", "arbitrary"),
            vmem_limit_bytes=_vmem_limit(vmem)),
        name="in_proj",
    )(*operands)
    if also_cast is None:
        return outs[0]
    return outs[0], outs[1].reshape(also_cast.shape)


POOL_SUB = 128


def _pool_kernel(u_ref, halo_ref, gate_ref, wg_ref, scale_ref, o_ref, pooled_ref):
    g = pl.program_id(0)
    i = pl.program_id(1)
    tm = u_ref.shape[0]
    w = lax.shift_left(jnp.int32(2), g)

    r = lax.broadcasted_iota(jnp.int32, (POOL_SUB, POOL_SUB), 0)
    c = lax.broadcasted_iota(jnp.int32, (POOL_SUB, POOL_SUB), 1)
    d_cur = r - c
    d_prev = d_cur + POOL_SUB
    band_cur = ((d_cur >= 0) & (d_cur < w)).astype(BF16)
    band_prev = (d_prev < w).astype(BF16)
    band = jnp.concatenate([band_prev, band_cur], axis=1)

    for rb in range(tm // POOL_SUB):
        rows = slice(rb * POOL_SUB, (rb + 1) * POOL_SUB)
        cur = u_ref[rows, :]
        if rb == 0:
            halo = halo_ref[...]
            prev_and_cur = jnp.concatenate([jnp.where(i > 0, halo, jnp.zeros_like(halo)), cur], axis=0)
        else:
            prev_and_cur = u_ref[(rb - 1) * POOL_SUB:(rb + 1) * POOL_SUB, :]
        wsum = jnp.dot(band, prev_and_cur, preferred_element_type=F32)
        t = i * tm + rb * POOL_SUB + lax.broadcasted_iota(jnp.int32, (POOL_SUB, 1), 0)
        inv_count = 1.0 / jnp.minimum(t + 1, w).astype(F32)
        pooled_ref[rows, :] = (wsum * inv_count - cur.astype(F32)).astype(pooled_ref.dtype)

    mixed = jnp.dot(pooled_ref[...], wg_ref[...].astype(BF16), preferred_element_type=F32)
    gate = gate_ref[...].astype(F32)
    o_ref[...] = (mixed * scale_ref[...] * (gate * jax.nn.sigmoid(gate))).astype(o_ref.dtype)


def _pool_mix(ug, w_group, scale, layer, *, tm=1024):
    s = ug.shape[0]
    _, n_groups, gd, _ = w_group.shape
    width = n_groups * gd
    sub_per_tile = tm // POOL_SUB
    vmem = (2 * (2 * tm * gd * 2 + POOL_SUB * gd * 2 + gd * gd * w_group.dtype.itemsize + tm * gd * 2)
            + tm * gd * 2 + gd * gd * 2 + 2 * tm * gd * 4)
    return pl.pallas_call(
        _pool_kernel,
        grid=(n_groups, s // tm),
        in_specs=[
            pl.BlockSpec((tm, gd), lambda g, i: (i, g)),
            pl.BlockSpec((POOL_SUB, gd), lambda g, i: (jnp.maximum(i * sub_per_tile - 1, 0), g)),
            pl.BlockSpec((tm, gd), lambda g, i: (i, n_groups + g)),
            pl.BlockSpec((None, None, gd, gd), lambda g, i: (layer, g, 0, 0)),
            pl.BlockSpec((1, gd), lambda g, i: (0, g)),
        ],
        out_specs=pl.BlockSpec((tm, gd), lambda g, i: (i, g)),
        out_shape=jax.ShapeDtypeStruct((s, width), BF16),
        scratch_shapes=[pltpu.VMEM((tm, gd), BF16)],
        compiler_params=pltpu.CompilerParams(
            dimension_semantics=("arbitrary", "arbitrary"),
            vmem_limit_bytes=_vmem_limit(vmem)),
        name="pool_mix",
    )(ug, ug, ug, w_group, scale.reshape(1, width))


def _out_proj_kernel(a_ref, w_hbm, x_ref, gpost_ref, *rest, tn, layer, n_row_tiles, n_h):
    gnext_refs = rest[:n_h]
    xo_ref = rest[n_h]
    ho_refs = rest[n_h + 1:2 * n_h + 1]
    work_ref, xfull_ref = rest[2 * n_h + 1:2 * n_h + 3]
    hdone_refs = rest[2 * n_h + 3:3 * n_h + 3]
    ring_ref, sem = rest[3 * n_h + 3:]
    row = pl.program_id(0)
    j = pl.program_id(1)
    tm = work_ref.shape[0]
    cols = pl.ds(pl.multiple_of(j * tn, tn), tn)

    @pl.when(row > 0)
    def _():
        xo_ref[...] = work_ref[:, cols]
        for ho_ref, hdone_ref in zip(ho_refs, hdone_refs):
            ho_ref[...] = hdone_ref[:, cols]

    @pl.when(row < n_row_tiles)
    def _():
        w_ref = _weight_slab(w_hbm, ring_ref, sem, layer, tn, n_row_tiles)
        work_ref[:, cols] = jnp.dot(a_ref[...], w_ref[...], preferred_element_type=F32)
        xfull_ref[:, cols] = x_ref[...]

    @pl.when((row < n_row_tiles) & (j == pl.num_programs(1) - 1))
    def _():
        gpost = gpost_ref[...]
        gnexts = [g_ref[...] for g_ref in gnext_refs]
        for c in range(tm // NORM_ROW_CHUNK):
            r = pl.ds(c * NORM_ROW_CHUNK, NORM_ROW_CHUNK)
            y = work_ref[r, :]
            xn = xfull_ref[r, :] + y * _rms_scale(y) * gpost
            work_ref[r, :] = xn
            if n_h:
                normed = xn * _rms_scale(xn)
                for gnext, hdone_ref in zip(gnexts, hdone_refs):
                    hdone_ref[r, :] = (normed * gnext).astype(hdone_ref.dtype)


def _out_proj(a, w, layer, x, g_post, g_next=(), *, tm=512, tn=512):
    s, kdim = a.shape
    d = w.shape[2]
    n_row_tiles, n_slabs = s // tm, d // tn
    n_h = len(g_next)
    kern = functools.partial(_out_proj_kernel, tn=tn, layer=layer, n_row_tiles=n_row_tiles, n_h=n_h)

    def in_row(i, j):
        return jnp.minimum(i, n_row_tiles - 1)

    def in_slab(i, j):
        return jnp.where(i < n_row_tiles, j, n_slabs - 1)

    def out_block(i, j):
        return jnp.maximum(i - 1, 0), jnp.where(i == 0, 0, j)

    out_dtypes = [F32] + [BF16] * n_h
    gain_spec = pl.BlockSpec((1, d), lambda i, j: (0, 0))
    vmem = (2 * tm * kdim * 2 + RING_DEPTH * kdim * tn * 2 + 2 * tm * tn * 4 + 2 * tm * d * 4
            + n_h * tm * d * 2 + sum(2 * tm * tn * jnp.dtype(t).itemsize for t in out_dtypes) + tm * tn * 4)
    outs = pl.pallas_call(
        kern,
        grid=(n_row_tiles + 1, n_slabs),
        in_specs=[
            pl.BlockSpec((tm, kdim), lambda i, j: (in_row(i, j), 0)),
            pl.BlockSpec(memory_space=pl.ANY),
            pl.BlockSpec((tm, tn), lambda i, j: (in_row(i, j), in_slab(i, j))),
            gain_spec,
        ] + [gain_spec] * n_h,
        out_specs=[pl.BlockSpec((tm, tn), out_block) for _ in out_dtypes],
        out_shape=[jax.ShapeDtypeStruct((s, d), t) for t in out_dtypes],
        scratch_shapes=([pltpu.VMEM((tm, d), F32), pltpu.VMEM((tm, d), F32)]
                        + [pltpu.VMEM((tm, d), BF16)] * n_h
                        + [pltpu.VMEM((RING_DEPTH, kdim, tn), w.dtype), pltpu.SemaphoreType.DMA((RING_DEPTH,))]),
        compiler_params=pltpu.CompilerParams(
            dimension_semantics=("arbitrary", "arbitrary"),
            vmem_limit_bytes=_vmem_limit(vmem)),
        name="out_proj",
    )(a, w, x, g_post.reshape(1, d), *[g.reshape(1, d) for g in g_next])
    return outs[0], list(outs[1:])


LOG2_E = math.log2(math.e)
SOFTPLUS2_LINEAR_ABOVE = 64.0
MASKED_EXPONENT = -1e30
SKIP_CARRY = 160.0


def _softplus2(z):
    return jnp.where(z > SOFTPLUS2_LINEAR_ABOVE, z, jnp.log(1.0 + jnp.exp2(z)) * LOG2_E)


def _sb_attn_kernel(q_ref, k_ref, v_ref, gate_ref, o_ref, acc_ref, carry_ref, sgate_ref, low_ref,
                    *, tq, bk, n_heads):
    i = pl.program_id(1)
    top = 2 * i + 1
    r = lax.broadcasted_iota(jnp.int32, (bk, bk), 0)
    c = lax.broadcasted_iota(jnp.int32, (bk, bk), 1)
    suffix = (r >= c).astype(BF16)
    before = c < r

    def key_rows(kb):
        return pl.ds(pl.multiple_of(kb * bk, bk), bk)

    def head_cols(head):
        return pl.ds(head * HEAD_DIM, HEAD_DIM)

    def scores(head, q_rows, kb, diagonal):
        z = lax.dot_general(q_ref[q_rows, head_cols(head)], k_ref[key_rows(kb), head_cols(head)],
                            (((1,), (1,)), ((), ())), preferred_element_type=F32)
        sp = _softplus2(z)
        if diagonal:
            sp = jnp.where(before, sp, 0.0)
        s_in = jnp.dot(sp.astype(BF16), suffix, preferred_element_type=F32)
        expo = jnp.minimum(z - s_in, 0.0)
        if diagonal:
            expo = jnp.where(before, expo, MASKED_EXPONENT)
        return expo, s_in[:, 0:1]

    def accumulate(head, q_rows, kb, expo, tot, first):
        v = v_ref[key_rows(kb), head_cols(head)]
        if first:
            acc_ref[q_rows, head_cols(head)] = jnp.dot(jnp.exp2(expo).astype(BF16), v, preferred_element_type=F32)
            carry_ref[head, q_rows, :] = tot
        else:
            carry = carry_ref[head, q_rows, :]
            a = jnp.exp2(expo - carry)
            acc_ref[q_rows, head_cols(head)] += jnp.dot(a.astype(BF16), v, preferred_element_type=F32)
            carry_ref[head, q_rows, :] = carry + tot

    def run_blocks(blocks):
        work = [(head,) + blk for blk in blocks for head in range(n_heads)]
        parts = [scores(*item) for item in work]
        for (head, q_rows, kb, diagonal), (expo, tot) in zip(work, parts):
            accumulate(head, q_rows, kb, expo, tot, first=diagonal)

    def min_carry():
        return jnp.min(carry_ref[...])

    early_rows = pl.ds(0, bk)
    late_rows = pl.ds(bk, bk)
    all_rows = pl.ds(0, tq)

    def first_blocks(blocks):
        run_blocks(blocks)
        low_ref[0] = min_carry()
        gate = gate_ref[...].astype(F32)
        sgate_ref[...] = gate * jax.nn.sigmoid(gate)

    @pl.when(i == 0)
    def _():
        first_blocks([(late_rows, top, True), (early_rows, top - 1, True), (late_rows, top - 1, False)])

    @pl.when(i > 0)
    def _():
        first_blocks([(late_rows, top, True), (early_rows, top - 1, True),
                      (late_rows, top - 1, False), (early_rows, top - 2, False)])

    @pl.when((i > 0) & (low_ref[0] < SKIP_CARRY))
    def _():
        run_blocks([(late_rows, top - 2, False)])
        low_ref[0] = min_carry()

    def keep_going(state):
        kb, lowest = state
        return (kb >= 0) & (lowest < SKIP_CARRY)

    def body(state):
        kb, _ = state
        run_blocks([(all_rows, kb, False)])
        return kb - 1, min_carry()

    lax.while_loop(keep_going, body, (top - 3, low_ref[0]))
    o_ref[...] = (acc_ref[...] * sgate_ref[...]).astype(o_ref.dtype)


def _sb_attention(qg, kv, *, bk=256, heads_per_step=4):
    s, two_w = qg.shape
    n_groups = (two_w // 2) // (heads_per_step * HEAD_DIM)
    tq = 2 * bk
    gw = heads_per_step * HEAD_DIM
    kern = functools.partial(_sb_attn_kernel, tq=tq, bk=bk, n_heads=heads_per_step)
    vmem = 2 * (2 * s * gw * 2 + 3 * tq * gw * 2) + 2 * tq * gw * 4 + heads_per_step * 16 * tq * bk * 4
    return pl.pallas_call(
        kern,
        grid=(n_groups, s // tq),
        in_specs=[
            pl.BlockSpec((tq, gw), lambda h, i: (i, h)),
            pl.BlockSpec((s, gw), lambda h, i: (0, h)),
            pl.BlockSpec((s, gw), lambda h, i: (0, n_groups + h)),
            pl.BlockSpec((tq, gw), lambda h, i: (i, n_groups + h)),
        ],
        out_specs=pl.BlockSpec((tq, gw), lambda h, i: (i, h)),
        out_shape=jax.ShapeDtypeStruct((s, n_groups * gw), BF16),
        scratch_shapes=[pltpu.VMEM((tq, gw), F32), pltpu.VMEM((heads_per_step, tq, 1), F32),
                        pltpu.VMEM((tq, gw), F32), pltpu.SMEM((1,), F32)],
        compiler_params=pltpu.CompilerParams(
            dimension_semantics=("arbitrary", "arbitrary"),
            vmem_limit_bytes=_vmem_limit(vmem)),
        name="sb_attention",
    )(qg, kv, kv, qg)


def kernel(x, a_pre_norm, a_w_in, a_w_group, a_scale, a_w_out, a_post_norm, kv_norm, w_kv,
           b_pre_norm, b_w_in, b_w_out, b_post_norm):
    b, s, d = x.shape
    assert b == 1
    n_a, n_b = a_w_in.shape[0], b_w_in.shape[0]
    w_kv = w_kv[None]

    xs = x.reshape(s, d)
    h = _norm(xs, a_pre_norm[0])
    for layer in range(n_a):
        if layer == 0:
            ug, a_w_out = _matmul(h, a_w_in, layer, also_cast=a_w_out)
        else:
            ug = _matmul(h, a_w_in, layer)
        mixed = _pool_mix(ug, a_w_group, a_scale[layer], layer)
        if layer + 1 < n_a:
            xs, (h,) = _out_proj(mixed, a_w_out, layer, xs, a_post_norm[layer], (a_pre_norm[layer + 1],))
        else:
            xs, (h, h_kv) = _out_proj(mixed, a_w_out, layer, xs, a_post_norm[layer], (b_pre_norm[0], kv_norm))
    kv, b_w_out = _matmul(h_kv, w_kv, 0, also_cast=b_w_out)
    for layer in range(n_b):
        qg = _matmul(h, b_w_in, layer, q_scale=LOG2_E / math.sqrt(HEAD_DIM))
        og = _sb_attention(qg, kv)
        g_next = (b_pre_norm[layer + 1],) if layer + 1 < n_b else ()
        xs, hs = _out_proj(og, b_w_out, layer, xs, b_post_norm[layer], g_next)
        h = hs[0] if hs else None
    return xs.reshape(b, s, d)
```

```python
import functools
import math

import jax
import jax.numpy as jnp
from jax import lax
from jax.experimental import pallas as pl
from jax.experimental.pallas import tpu as pltpu

RMS_EPS = 1e-6
HEAD_DIM = 128

V7X_VMEM_BYTES = 64 * 1024 * 1024

BF16 = jnp.bfloat16
F32 = jnp.float32
NORM_ROW_CHUNK = 32
SLAB = 512


def _vmem_limit(nbytes):
    return int(min(V7X_VMEM_BYTES - 6 * 1024 * 1024, nbytes + 8 * 1024 * 1024))


def _rms_scale(x):
    return lax.rsqrt(jnp.mean(x * x, axis=-1, keepdims=True) + RMS_EPS)


RING_DEPTH = 3


def _weight_slab(w_hbm, ring_ref, sem, layer, tn, n_rows):
    n_slabs = pl.num_programs(1)
    step = pl.program_id(0) * n_slabs + pl.program_id(1)
    n_steps = n_rows * n_slabs

    def fetch(s):
        start = (s % n_slabs) * tn
        cols = pl.ds(start if isinstance(start, int) else pl.multiple_of(start, tn), tn)
        slot = s % RING_DEPTH
        return pltpu.make_async_copy(w_hbm.at[layer, :, cols], ring_ref.at[slot], sem.at[slot])

    @pl.when(step == 0)
    def _():
        for s in range(RING_DEPTH - 1):
            fetch(s).start()

    @pl.when(step + (RING_DEPTH - 1) < n_steps)
    def _():
        fetch(step + (RING_DEPTH - 1)).start()

    fetch(step).wait()
    return ring_ref.at[step % RING_DEPTH]


def _norm_kernel(x_ref, g_ref, o_ref):
    x = x_ref[...]
    o_ref[...] = (x * _rms_scale(x) * g_ref[...]).astype(o_ref.dtype)


def _norm(x, g, *, tm=256):
    s, d = x.shape
    return pl.pallas_call(
        _norm_kernel,
        grid=(s // tm,),
        in_specs=[pl.BlockSpec((tm, d), lambda i: (i, 0)), pl.BlockSpec((1, d), lambda i: (0, 0))],
        out_specs=pl.BlockSpec((tm, d), lambda i: (i, 0)),
        out_shape=jax.ShapeDtypeStruct((s, d), BF16),
        compiler_params=pltpu.CompilerParams(dimension_semantics=("arbitrary",)),
        name="norm",
    )(x, g.reshape(1, d))


def _matmul_kernel(h_ref, w_hbm, *rest, q_tiles, q_scale, layer):
    *io_refs, ring_ref, sem = rest
    if len(io_refs) == 3:
        cast_src_ref, o_ref, cast_dst_ref = io_refs
        cast_dst_ref[...] = cast_src_ref[...].astype(cast_dst_ref.dtype)
    else:
        (o_ref,) = io_refs
    w_ref = _weight_slab(w_hbm, ring_ref, sem, layer, o_ref.shape[1], pl.num_programs(0))
    acc = jnp.dot(h_ref[...], w_ref[...].astype(h_ref.dtype), preferred_element_type=F32)
    if q_tiles:
        acc = acc * jnp.where(pl.program_id(1) < q_tiles, q_scale, 1.0).astype(F32)
    o_ref[...] = acc.astype(o_ref.dtype)


def _matmul(h, w, layer, *, q_scale=None, also_cast=None, slab_major=False, tm=1024, tn=SLAB):
    s, d = h.shape
    n = w.shape[2]
    n_i, n_j = s // tm, n // tn
    q_tiles = (n // 2) // tn if q_scale is not None else 0
    kern = functools.partial(_matmul_kernel, q_tiles=q_tiles, q_scale=q_scale if q_scale is not None else 1.0,
                             layer=layer)
    vmem = (2 * tm * d * 2 + RING_DEPTH * d * tn * w.dtype.itemsize + d * tn * 2 + 2 * tm * tn * 2
            + tm * tn * 4)
    operands = [h, w]
    in_specs = [
        pl.BlockSpec((tm, d), lambda i, j: (i, 0)),
        pl.BlockSpec(memory_space=pl.ANY),
    ]
    if slab_major:
        out_specs = [pl.BlockSpec((None, tm, tn), lambda i, j: (j, i, 0))]
        out_shape = [jax.ShapeDtypeStruct((n_j, s, tn), BF16)]
    else:
        out_specs = [pl.BlockSpec((tm, tn), lambda i, j: (i, j))]
        out_shape = [jax.ShapeDtypeStruct((s, n), BF16)]
    if also_cast is not None:
        cols = also_cast.shape[-1]
        flat = also_cast.reshape(-1, cols)
        rows_per_step = flat.shape[0] // (n_i * n_j)
        assert rows_per_step * n_i * n_j == flat.shape[0]
        operands.append(flat)
        cast_spec = pl.BlockSpec((rows_per_step, cols), lambda i, j: (i * n_j + j, 0))
        in_specs.append(cast_spec)
        out_specs.append(cast_spec)
        out_shape.append(jax.ShapeDtypeStruct(flat.shape, BF16))
        vmem += 2 * rows_per_step * cols * (4 + 2)
    outs = pl.pallas_call(
        kern,
        grid=(n_i, n_j),
        in_specs=in_specs,
        out_specs=out_specs,
        out_shape=out_shape,
        scratch_shapes=[pltpu.VMEM((RING_DEPTH, d, tn), w.dtype), pltpu.SemaphoreType.DMA((RING_DEPTH,))],
        compiler_params=pltpu.CompilerParams(
            dimension_semantics=("arbitrary", "arbitrary"),
            vmem_limit_bytes=_vmem_limit(vmem)),
        name="in_proj",
    )(*operands)
    if also_cast is None:
        return outs[0]
    return outs[0], outs[1].reshape(also_cast.shape)


POOL_SUB = 128


def _pool_kernel(u_ref, halo_ref, gate_ref, wg_ref, scale_ref, o_ref, pooled_ref):
    g = pl.program_id(0)
    i = pl.program_id(1)
    tm = u_ref.shape[0]
    w = lax.shift_left(jnp.int32(2), g)

    r = lax.broadcasted_iota(jnp.int32, (POOL_SUB, POOL_SUB), 0)
    c = lax.broadcasted_iota(jnp.int32, (POOL_SUB, POOL_SUB), 1)
    d_cur = r - c
    d_prev = d_cur + POOL_SUB
    band_cur = ((d_cur >= 0) & (d_cur < w)).astype(BF16)
    band_prev = (d_prev < w).astype(BF16)
    band = jnp.concatenate([band_prev, band_cur], axis=1)

    for rb in range(tm // POOL_SUB):
        rows = slice(rb * POOL_SUB, (rb + 1) * POOL_SUB)
        cur = u_ref[rows, :]
        if rb == 0:
            halo = halo_ref[...]
            prev_and_cur = jnp.concatenate([jnp.where(i > 0, halo, jnp.zeros_like(halo)), cur], axis=0)
        else:
            prev_and_cur = u_ref[(rb - 1) * POOL_SUB:(rb + 1) * POOL_SUB, :]
        wsum = jnp.dot(band, prev_and_cur, preferred_element_type=F32)
        t = i * tm + rb * POOL_SUB + lax.broadcasted_iota(jnp.int32, (POOL_SUB, 1), 0)
        inv_count = 1.0 / jnp.minimum(t + 1, w).astype(F32)
        pooled_ref[rows, :] = (wsum * inv_count - cur.astype(F32)).astype(pooled_ref.dtype)

    mixed = jnp.dot(pooled_ref[...], wg_ref[...].astype(BF16), preferred_element_type=F32)
    gate = gate_ref[...].astype(F32)
    o_ref[...] = (mixed * scale_ref[...] * (gate * jax.nn.sigmoid(gate))).astype(o_ref.dtype)


def _pool_mix(ug, w_group, scale, layer, *, tm=1024):
    s = ug.shape[0]
    _, n_groups, gd, _ = w_group.shape
    width = n_groups * gd
    sub_per_tile = tm // POOL_SUB
    vmem = (2 * (2 * tm * gd * 2 + POOL_SUB * gd * 2 + gd * gd * w_group.dtype.itemsize + tm * gd * 2)
            + tm * gd * 2 + gd * gd * 2 + 2 * tm * gd * 4)
    return pl.pallas_call(
        _pool_kernel,
        grid=(n_groups, s // tm),
        in_specs=[
            pl.BlockSpec((tm, gd), lambda g, i: (i, g)),
            pl.BlockSpec((POOL_SUB, gd), lambda g, i: (jnp.maximum(i * sub_per_tile - 1, 0), g)),
            pl.BlockSpec((tm, gd), lambda g, i: (i, n_groups + g)),
            pl.BlockSpec((None, None, gd, gd), lambda g, i: (layer, g, 0, 0)),
            pl.BlockSpec((1, gd), lambda g, i: (0, g)),
        ],
        out_specs=pl.BlockSpec((tm, gd), lambda g, i: (i, g)),
        out_shape=jax.ShapeDtypeStruct((s, width), BF16),
        scratch_shapes=[pltpu.VMEM((tm, gd), BF16)],
        compiler_params=pltpu.CompilerParams(
            dimension_semantics=("arbitrary", "arbitrary"),
            vmem_limit_bytes=_vmem_limit(vmem)),
        name="pool_mix",
    )(ug, ug, ug, w_group, scale.reshape(1, width))


def _out_proj_kernel(a_ref, w_hbm, x_ref, gpost_ref, *rest, tn, layer, n_row_tiles, n_h):
    gnext_refs = rest[:n_h]
    xo_ref = rest[n_h]
    ho_refs = rest[n_h + 1:2 * n_h + 1]
    work_ref, xfull_ref = rest[2 * n_h + 1:2 * n_h + 3]
    hdone_refs = rest[2 * n_h + 3:3 * n_h + 3]
    ring_ref, sem = rest[3 * n_h + 3:]
    row = pl.program_id(0)
    j = pl.program_id(1)
    tm = work_ref.shape[0]
    cols = pl.ds(pl.multiple_of(j * tn, tn), tn)

    @pl.when(row > 0)
    def _():
        xo_ref[...] = work_ref[:, cols]
        for ho_ref, hdone_ref in zip(ho_refs, hdone_refs):
            ho_ref[...] = hdone_ref[:, cols]

    @pl.when(row < n_row_tiles)
    def _():
        w_ref = _weight_slab(w_hbm, ring_ref, sem, layer, tn, n_row_tiles)
        work_ref[:, cols] = jnp.dot(a_ref[...], w_ref[...], preferred_element_type=F32)
        xfull_ref[:, cols] = x_ref[...]

    @pl.when((row < n_row_tiles) & (j == pl.num_programs(1) - 1))
    def _():
        gpost = gpost_ref[...]
        gnexts = [g_ref[...] for g_ref in gnext_refs]
        for c in range(tm // NORM_ROW_CHUNK):
            r = pl.ds(c * NORM_ROW_CHUNK, NORM_ROW_CHUNK)
            y = work_ref[r, :]
            xn = xfull_ref[r, :] + y * _rms_scale(y) * gpost
            work_ref[r, :] = xn
            if n_h:
                normed = xn * _rms_scale(xn)
                for gnext, hdone_ref in zip(gnexts, hdone_refs):
                    hdone_ref[r, :] = (normed * gnext).astype(hdone_ref.dtype)


def _out_proj(a, w, layer, x, g_post, g_next=(), *, tm=512, tn=512):
    s, kdim = a.shape
    d = w.shape[2]
    n_row_tiles, n_slabs = s // tm, d // tn
    n_h = len(g_next)
    kern = functools.partial(_out_proj_kernel, tn=tn, layer=layer, n_row_tiles=n_row_tiles, n_h=n_h)

    def in_row(i, j):
        return jnp.minimum(i, n_row_tiles - 1)

    def in_slab(i, j):
        return jnp.where(i < n_row_tiles, j, n_slabs - 1)

    def out_block(i, j):
        return jnp.maximum(i - 1, 0), jnp.where(i == 0, 0, j)

    out_dtypes = [F32] + [BF16] * n_h
    gain_spec = pl.BlockSpec((1, d), lambda i, j: (0, 0))
    vmem = (2 * tm * kdim * 2 + RING_DEPTH * kdim * tn * 2 + 2 * tm * tn * 4 + 2 * tm * d * 4
            + n_h * tm * d * 2 + sum(2 * tm * tn * jnp.dtype(t).itemsize for t in out_dtypes) + tm * tn * 4)
    outs = pl.pallas_call(
        kern,
        grid=(n_row_tiles + 1, n_slabs),
        in_specs=[
            pl.BlockSpec((tm, kdim), lambda i, j: (in_row(i, j), 0)),
            pl.BlockSpec(memory_space=pl.ANY),
            pl.BlockSpec((tm, tn), lambda i, j: (in_row(i, j), in_slab(i, j))),
            gain_spec,
        ] + [gain_spec] * n_h,
        out_specs=[pl.BlockSpec((tm, tn), out_block) for _ in out_dtypes],
        out_shape=[jax.ShapeDtypeStruct((s, d), t) for t in out_dtypes],
        scratch_shapes=([pltpu.VMEM((tm, d), F32), pltpu.VMEM((tm, d), F32)]
                        + [pltpu.VMEM((tm, d), BF16)] * n_h
                        + [pltpu.VMEM((RING_DEPTH, kdim, tn), w.dtype), pltpu.SemaphoreType.DMA((RING_DEPTH,))]),
        compiler_params=pltpu.CompilerParams(
            dimension_semantics=("arbitrary", "arbitrary"),
            vmem_limit_bytes=_vmem_limit(vmem)),
        name="out_proj",
    )(a, w, x, g_post.reshape(1, d), *[g.reshape(1, d) for g in g_next])
    return outs[0], list(outs[1:])


LOG2_E = math.log2(math.e)
SOFTPLUS2_LINEAR_ABOVE = 64.0
MASKED_EXPONENT = -1e30
SKIP_CARRY = 160.0


def _softplus2(z):
    return jnp.where(z > SOFTPLUS2_LINEAR_ABOVE, z, jnp.log(1.0 + jnp.exp2(z)) * LOG2_E)


def _sb_attn_kernel(q_ref, k_ref, v_ref, gate_ref, o_ref, acc_ref, carry_ref, sgate_ref, low_ref,
                    *, tq, bk, n_heads):
    i = pl.program_id(1)
    top = 2 * i + 1
    r = lax.broadcasted_iota(jnp.int32, (bk, bk), 0)
    c = lax.broadcasted_iota(jnp.int32, (bk, bk), 1)
    suffix = (r >= c).astype(BF16)
    before = c < r

    def key_rows(kb):
        return pl.ds(pl.multiple_of(kb * bk, bk), bk)

    def head_cols(head):
        return pl.ds(head * HEAD_DIM, HEAD_DIM)

    def scores(head, q_rows, kb, diagonal):
        z = lax.dot_general(q_ref[q_rows, head_cols(head)], k_ref[key_rows(kb), head_cols(head)],
                            (((1,), (1,)), ((), ())), preferred_element_type=F32)
        sp = _softplus2(z)
        if diagonal:
            sp = jnp.where(before, sp, 0.0)
        s_in = jnp.dot(sp.astype(BF16), suffix, preferred_element_type=F32)
        expo = jnp.minimum(z - s_in, 0.0)
        if diagonal:
            expo = jnp.where(before, expo, MASKED_EXPONENT)
        return expo, s_in[:, 0:1]

    def accumulate(head, q_rows, kb, expo, tot, first):
        v = v_ref[key_rows(kb), head_cols(head)]
        if first:
            acc_ref[q_rows, head_cols(head)] = jnp.dot(jnp.exp2(expo).astype(BF16), v, preferred_element_type=F32)
            carry_ref[head, q_rows, :] = tot
        else:
            carry = carry_ref[head, q_rows, :]
            a = jnp.exp2(expo - carry)
            acc_ref[q_rows, head_cols(head)] += jnp.dot(a.astype(BF16), v, preferred_element_type=F32)
            carry_ref[head, q_rows, :] = carry + tot

    def run_blocks(blocks):
        work = [(head,) + blk for blk in blocks for head in range(n_heads)]
        parts = [scores(*item) for item in work]
        for (head, q_rows, kb, diagonal), (expo, tot) in zip(work, parts):
            accumulate(head, q_rows, kb, expo, tot, first=diagonal)

    def min_carry():
        return jnp.min(carry_ref[...])

    early_rows = pl.ds(0, bk)
    late_rows = pl.ds(bk, bk)
    all_rows = pl.ds(0, tq)

    def first_blocks(blocks):
        run_blocks(blocks)
        low_ref[0] = min_carry()
        gate = gate_ref[...].astype(F32)
        sgate_ref[...] = gate * jax.nn.sigmoid(gate)

    @pl.when(i == 0)
    def _():
        first_blocks([(late_rows, top, True), (early_rows, top - 1, True), (late_rows, top - 1, False)])

    @pl.when(i > 0)
    def _():
        first_blocks([(late_rows, top, True), (early_rows, top - 1, True),
                      (late_rows, top - 1, False), (early_rows, top - 2, False)])

    @pl.when((i > 0) & (low_ref[0] < SKIP_CARRY))
    def _():
        run_blocks([(late_rows, top - 2, False)])
        low_ref[0] = min_carry()

    def keep_going(state):
        kb, lowest = state
        return (kb >= 0) & (lowest < SKIP_CARRY)

    def body(state):
        kb, _ = state
        run_blocks([(all_rows, kb, False)])
        return kb - 1, min_carry()

    lax.while_loop(keep_going, body, (top - 3, low_ref[0]))
    o_ref[...] = (acc_ref[...] * sgate_ref[...]).astype(o_ref.dtype)


def _sb_attention(qg, kv, *, bk=256):
    n_slabs, s, gw = qg.shape
    n_groups = n_slabs // 2
    heads_per_step = gw // HEAD_DIM
    tq = 2 * bk
    kern = functools.partial(_sb_attn_kernel, tq=tq, bk=bk, n_heads=heads_per_step)
    vmem = 2 * (2 * s * gw * 2 + 3 * tq * gw * 2) + 2 * tq * gw * 4 + heads_per_step * 16 * tq * bk * 4
    return pl.pallas_call(
        kern,
        grid=(n_groups, s // tq),
        in_specs=[
            pl.BlockSpec((None, tq, gw), lambda h, i: (h, i, 0)),
            pl.BlockSpec((None, s, gw), lambda h, i: (h, 0, 0)),
            pl.BlockSpec((None, s, gw), lambda h, i: (n_groups + h, 0, 0)),
            pl.BlockSpec((None, tq, gw), lambda h, i: (n_groups + h, i, 0)),
        ],
        out_specs=pl.BlockSpec((tq, gw), lambda h, i: (i, h)),
        out_shape=jax.ShapeDtypeStruct((s, n_groups * gw), BF16),
        scratch_shapes=[pltpu.VMEM((tq, gw), F32), pltpu.VMEM((heads_per_step, tq, 1), F32),
                        pltpu.VMEM((tq, gw), F32), pltpu.SMEM((1,), F32)],
        compiler_params=pltpu.CompilerParams(
            dimension_semantics=("arbitrary", "arbitrary"),
            vmem_limit_bytes=_vmem_limit(vmem)),
        name="sb_attention",
    )(qg, kv, kv, qg)


def kernel(x, a_pre_norm, a_w_in, a_w_group, a_scale, a_w_out, a_post_norm, kv_norm, w_kv,
           b_pre_norm, b_w_in, b_w_out, b_post_norm):
    b, s, d = x.shape
    assert b == 1
    n_a, n_b = a_w_in.shape[0], b_w_in.shape[0]
    w_kv = w_kv[None]

    xs = x.reshape(s, d)
    h = _norm(xs, a_pre_norm[0])
    for layer in range(n_a):
        if layer == 0:
            ug, a_w_out = _matmul(h, a_w_in, layer, also_cast=a_w_out)
        else:
            ug = _matmul(h, a_w_in, layer)
        mixed = _pool_mix(ug, a_w_group, a_scale[layer], layer)
        if layer + 1 < n_a:
            xs, (h,) = _out_proj(mixed, a_w_out, layer, xs, a_post_norm[layer], (a_pre_norm[layer + 1],))
        else:
            xs, (h, h_kv) = _out_proj(mixed, a_w_out, layer, xs, a_post_norm[layer], (b_pre_norm[0], kv_norm))
    kv, b_w_out = _matmul(h_kv, w_kv, 0, also_cast=b_w_out, slab_major=True)
    for layer in range(n_b):
        qg = _matmul(h, b_w_in, layer, q_scale=LOG2_E / math.sqrt(HEAD_DIM), slab_major=True)
        og = _sb_attention(qg, kv)
        g_next = (b_pre_norm[layer + 1],) if layer + 1 < n_b else ()
        xs, hs = _out_proj(og, b_w_out, layer, xs, b_post_norm[layer], g_next)
        h = hs[0] if hs else None
    return xs.reshape(b, s, d)
```
